```python
import math
import jax, jax.numpy as jnp
from jax import lax
import numpy as np

D_MODEL = 1024
BATCH = 2
SEQ = 8192
DEPTH = 2
DEC_BATCH = 32
DEC_SEQ = 64
PAST_LEN = 1024

CHUNK = 64
SB_HEADS = 8
SB_HEAD_DIM = 128
SB_WIDTH = SB_HEADS * SB_HEAD_DIM
Q_BLOCK = 128
HG_HEADS = 8
HG_KEY_DIM = 128
HG_VAL_DIM = 128
HG_KEY_WIDTH = HG_HEADS * HG_KEY_DIM
HG_VAL_WIDTH = HG_HEADS * HG_VAL_DIM
HG_BLOCK = CHUNK // 4
LB_FLOOR = 1e-30
N_EXPERTS = 64
TOP_K = 8
N_GROUPS = 8
TOPK_GROUPS = 4
D_EXPERT = 256
D_SHARED = 256
ROUTE_SCALE = 2.5
EXPERT_BLOCK = 128
NEG_BIG = -1e9
ALPHA = (2 * DEPTH) ** 0.25
BETA = (8 * DEPTH) ** -0.25
LN_EPS = 1e-5
RMS_EPS = 1e-6
IN_SPLITS = (SB_WIDTH, SB_WIDTH, SB_WIDTH, HG_KEY_WIDTH, HG_KEY_WIDTH, HG_VAL_WIDTH, HG_VAL_WIDTH, D_MODEL, D_MODEL)
IN_WIDTH = sum(IN_SPLITS)

kernel_name = "stickbreak_hgrn2_moe_deepnorm_stream"

F32 = jnp.float32


def layer_norm(x, g, b):
    xf = x.astype(F32)
    mu = jnp.mean(xf, -1, keepdims=True)
    var = jnp.mean(jnp.square(xf - mu), -1, keepdims=True)
    return ((xf - mu) * lax.rsqrt(var + LN_EPS) * g.astype(F32) + b.astype(F32)).astype(x.dtype)


def rms_norm(x, g):
    xf = x.astype(F32)
    return xf * lax.rsqrt(jnp.mean(xf * xf, -1, keepdims=True) + RMS_EPS) * g.astype(F32)


def stick_breaking_block(q, k, v, q_pos, k_pos):
    z = jnp.einsum('bqhd,bkhd->bhqk', q, k).astype(F32) / math.sqrt(SB_HEAD_DIM)
    mask = k_pos[None, :] < q_pos[:, None]
    log_keep = jnp.where(mask, jax.nn.log_sigmoid(-z), 0.0)
    after = lax.cumsum(log_keep, axis=3, reverse=True) - log_keep
    log_w = jnp.where(mask, jax.nn.log_sigmoid(z) + after, 0.0)
    w = jnp.where(mask, jnp.exp(log_w), 0.0)
    return jnp.einsum('bhqk,bkhd->bqhd', w.astype(v.dtype), v)


def stick_breaking_prompt(q, k, v):
    B, S, H, Dh = q.shape
    nb = S // Q_BLOCK
    k_pos = jnp.arange(S)
    qb = q.reshape(B, nb, Q_BLOCK, H, Dh).transpose(1, 0, 2, 3, 4)

    def one_block(args):
        qi, i = args
        q_pos = i * Q_BLOCK + jnp.arange(Q_BLOCK)
        return stick_breaking_block(qi, k, v, q_pos, k_pos)

    out = lax.map(one_block, (qb, jnp.arange(nb)))
    return out.transpose(1, 0, 2, 3, 4).reshape(B, S, H, Dh)


def hgrn2_chunked(q, logf, kk, v, s0):
    B, T, H, _ = q.shape
    pad = (-T) % HG_BLOCK
    padw = ((0, 0), (0, pad), (0, 0), (0, 0))
    q, logf, kk, v = [jnp.pad(a, padw) for a in (q, logf, kk, v)]
    nb = (T + pad) // HG_BLOCK

    def blocks(a):
        return a.reshape(B, nb, HG_BLOCK, H, a.shape[-1]).transpose(1, 0, 3, 2, 4)

    causal = jnp.tril(jnp.ones((HG_BLOCK, HG_BLOCK), bool))[:, :, None]

    def step(s, xs):
        qb, fb, kb, vb = xs
        b = jnp.cumsum(fb, axis=2)
        diff = b[:, :, :, None, :] - b[:, :, None, :, :]
        decay = jnp.where(causal, jnp.exp(jnp.where(causal, diff, 0.0)), 0.0)
        scores = jnp.einsum('bhtc,bhsc,bhtsc->bhts', qb, kb, decay)
        o = jnp.einsum('bhts,bhsv->bhtv', scores, vb) + jnp.einsum('bhtc,bhcv->bhtv', qb * jnp.exp(b), s)
        b_last = b[:, :, -1:, :]
        s_new = jnp.exp(b_last[:, :, 0, :])[..., None] * s + jnp.einsum('bhsc,bhsv->bhcv', kb * jnp.exp(b_last - b), vb)
        return s_new, o

    s_fin, o = lax.scan(step, s0, (blocks(q), blocks(logf), blocks(kk), blocks(v)))
    o = o.transpose(1, 0, 3, 2, 4).reshape(B, nb * HG_BLOCK, H, -1)[:, :T]
    return o, s_fin


def route(h2, w_router, router_bias):
    T = h2.shape[0]
    s = jax.nn.sigmoid(jnp.einsum('td,de->te', h2, w_router).astype(F32))
    sb = s + router_bias.astype(F32)
    grp = sb.reshape(T, N_GROUPS, N_EXPERTS // N_GROUPS)
    grp_score = jnp.sum(lax.top_k(grp, 2)[0], -1)
    _, gidx = lax.top_k(grp_score, TOPK_GROUPS)
    gmask = jnp.any(gidx[..., None] == jnp.arange(N_GROUPS), axis=-2)
    emask = jnp.repeat(gmask, N_EXPERTS // N_GROUPS, axis=-1)
    _, idx = lax.top_k(jnp.where(emask, sb, NEG_BIG), TOP_K)
    w = jnp.take_along_axis(s, idx, axis=-1)
    w = w / jnp.sum(w, -1, keepdims=True) * ROUTE_SCALE
    return idx, w


def routed_experts(h2, idx, wts, w_gate, w_up, w_down):
    T = h2.shape[0]
    A = T * TOP_K
    flat_e = idx.reshape(A)
    order = jnp.argsort(flat_e)
    sorted_e = flat_e[order]
    tok_sorted = (order // TOP_K).astype(jnp.int32)
    w_sorted = wts.reshape(A)[order].astype(h2.dtype)
    counts = jnp.zeros((N_EXPERTS,), jnp.int32).at[flat_e].add(1)
    padded = (counts + EXPERT_BLOCK - 1) // EXPERT_BLOCK * EXPERT_BLOCK
    pad_end = jnp.cumsum(padded)
    pad_start = pad_end - padded
    grp_start = jnp.cumsum(counts) - counts
    dest = pad_start[sorted_e] + jnp.arange(A, dtype=jnp.int32) - grp_start[sorted_e]
    n_blocks = -(-(A + N_EXPERTS * (EXPERT_BLOCK - 1)) // EXPERT_BLOCK)
    P = n_blocks * EXPERT_BLOCK
    row_tok = jnp.zeros((P,), jnp.int32).at[dest].set(tok_sorted)
    row_w = jnp.zeros((P,), h2.dtype).at[dest].set(w_sorted)
    block_e = jnp.minimum(jnp.searchsorted(pad_end, jnp.arange(n_blocks, dtype=jnp.int32) * EXPERT_BLOCK, side='right'), N_EXPERTS - 1)

    def step(y, xs):
        toks, ws, e = xs
        xb = h2[toks]
        hb = jax.nn.silu(xb @ w_gate[e]) * (xb @ w_up[e])
        return y.at[toks].add((hb @ w_down[e]) * ws[:, None]), None

    y, _ = lax.scan(step, jnp.zeros_like(h2), (row_tok.reshape(n_blocks, EXPERT_BLOCK), row_w.reshape(n_blocks, EXPERT_BLOCK), block_e))
    return y


def trunk_layer(x, past_k, past_v, s0, lb, w_in, w_out, hg_norm, ln1_g, ln1_b, w_router, router_bias,
                w_eg, w_eu, w_ed, w_sg, w_su, w_sd, ln2_g, ln2_b):
    B, T, _ = x.shape
    split_pts = tuple(int(c) for c in np.cumsum(IN_SPLITS)[:-1])
    sb_q, sb_k, sb_v, hg_q, hg_f, hg_i, hg_g, gate_sb, gate_hg = jnp.split(
        jnp.einsum('btd,de->bte', x, w_in), split_pts, axis=-1)
    q = sb_q.reshape(B, T, SB_HEADS, SB_HEAD_DIM)
    k = sb_k.reshape(B, T, SB_HEADS, SB_HEAD_DIM)
    v = sb_v.reshape(B, T, SB_HEADS, SB_HEAD_DIM)
    if past_k is None:
        o_sb = stick_breaking_prompt(q, k, v)
    else:
        P = past_k.shape[1]
        o_sb = stick_breaking_block(q, jnp.concatenate([past_k.astype(k.dtype), k], 1),
                                    jnp.concatenate([past_v.astype(v.dtype), v], 1),
                                    P + jnp.arange(T), jnp.arange(P + T))
    qh = jax.nn.silu(hg_q.astype(F32)).reshape(B, T, HG_HEADS, HG_KEY_DIM)
    a = hg_f.astype(F32)
    logf = jnp.logaddexp(jax.nn.log_sigmoid(a),
                         jax.nn.log_sigmoid(-a) + jnp.log(jnp.maximum(lb, LB_FLOOR)))
    logf = jnp.minimum(logf, 0.0).reshape(B, T, HG_HEADS, HG_KEY_DIM)
    kh = -jnp.expm1(logf)
    vh = hg_i.astype(F32).reshape(B, T, HG_HEADS, HG_VAL_DIM)
    o_hg, s_fin = hgrn2_chunked(qh, logf, kh, vh, s0.astype(F32))
    o_hg = rms_norm(o_hg, hg_norm) * jax.nn.silu(hg_g.astype(F32)).reshape(B, T, HG_HEADS, HG_VAL_DIM)
    merged = (jax.nn.sigmoid(gate_sb) * o_sb.reshape(B, T, SB_WIDTH)
              + jax.nn.sigmoid(gate_hg) * o_hg.reshape(B, T, HG_VAL_WIDTH).astype(x.dtype))
    x = layer_norm(ALPHA * x + jnp.einsum('bte,ed->btd', merged, w_out), ln1_g, ln1_b)
    h2 = x.reshape(B * T, D_MODEL)
    idx, wts = route(h2, w_router, router_bias)
    shared = (jax.nn.silu(h2 @ w_sg) * (h2 @ w_su)) @ w_sd
    ffn = routed_experts(h2, idx, wts, w_eg, w_eu, w_ed) + shared
    x = layer_norm(ALPHA * x + ffn.reshape(B, T, D_MODEL), ln2_g, ln2_b)
    return x, k, v, s_fin.astype(x.dtype)


def setup_inputs(seed: int = 0) -> dict:
    key = jax.random.key(seed)
    ks = jax.random.split(key, 24)

    def nrm(k, shape, scale):
        return jax.random.normal(k, shape, F32) * scale

    col_scale = jnp.asarray(np.concatenate(
        [np.full((s,), BETA if i in (2, 5) else 1.0, np.float32) for i, s in enumerate(IN_SPLITS)]))
    return {
        "x_prompt": nrm(ks[0], (BATCH, SEQ, D_MODEL), 1.0),
        "x_sample": nrm(ks[1], (DEC_BATCH, DEC_SEQ, D_MODEL), 1.0),
        "cache_sb_k": nrm(ks[2], (DEPTH, DEC_BATCH, PAST_LEN, SB_HEADS, SB_HEAD_DIM), 1.0),
        "cache_sb_v": nrm(ks[3], (DEPTH, DEC_BATCH, PAST_LEN, SB_HEADS, SB_HEAD_DIM), BETA),
        "state_hgrn": nrm(ks[4], (DEPTH, DEC_BATCH, HG_HEADS, HG_KEY_DIM, HG_VAL_DIM), 0.5),
        "ln_in_g": 1.0 + nrm(ks[5], (D_MODEL,), 0.02),
        "ln_in_b": nrm(ks[6], (D_MODEL,), 0.02),
        "w_in": nrm(ks[7], (DEPTH, D_MODEL, IN_WIDTH), D_MODEL ** -0.5) * col_scale,
        "w_out": nrm(ks[8], (DEPTH, SB_WIDTH, D_MODEL), SB_WIDTH ** -0.5 * BETA),
        "hg_norm": 1.0 + nrm(ks[9], (DEPTH, HG_VAL_DIM), 0.02),
        "hg_lb_logits": nrm(ks[10], (DEPTH, HG_KEY_WIDTH), 0.1),
        "ln1_g": 1.0 + nrm(ks[11], (DEPTH, D_MODEL), 0.02),
        "ln1_b": nrm(ks[12], (DEPTH, D_MODEL), 0.02),
        "w_router": nrm(ks[13], (DEPTH, D_MODEL, N_EXPERTS), D_MODEL ** -0.5),
        "router_bias": nrm(ks[14], (DEPTH, N_EXPERTS), 0.01),
        "w_exp_gate": nrm(ks[15], (DEPTH, N_EXPERTS, D_MODEL, D_EXPERT), D_MODEL ** -0.5),
        "w_exp_up": nrm(ks[16], (DEPTH, N_EXPERTS, D_MODEL, D_EXPERT), D_MODEL ** -0.5),
        "w_exp_down": nrm(ks[17], (DEPTH, N_EXPERTS, D_EXPERT, D_MODEL), D_EXPERT ** -0.5 * BETA),
        "w_sh_gate": nrm(ks[18], (DEPTH, D_MODEL, D_SHARED), D_MODEL ** -0.5),
        "w_sh_up": nrm(ks[19], (DEPTH, D_MODEL, D_SHARED), D_MODEL ** -0.5),
        "w_sh_down": nrm(ks[20], (DEPTH, D_SHARED, D_MODEL), D_SHARED ** -0.5 * BETA),
        "ln2_g": 1.0 + nrm(ks[21], (DEPTH, D_MODEL), 0.02),
        "ln2_b": nrm(ks[22], (DEPTH, D_MODEL), 0.02),
    }


def reference(x_prompt, x_sample, cache_sb_k, cache_sb_v, state_hgrn, ln_in_g, ln_in_b, w_in, w_out,
              hg_norm, hg_lb_logits, ln1_g, ln1_b, w_router, router_bias, w_exp_gate, w_exp_up,
              w_exp_down, w_sh_gate, w_sh_up, w_sh_down, ln2_g, ln2_b):
    lb_soft = jax.nn.softmax(hg_lb_logits.astype(F32), axis=0)
    lower_bounds = jnp.cumsum(lb_soft, axis=0) - lb_soft[0]
    xp = layer_norm(x_prompt, ln_in_g, ln_in_b)
    xs = layer_norm(x_sample, ln_in_g, ln_in_b)
    s0_prompt = jnp.zeros((x_prompt.shape[0], HG_HEADS, HG_KEY_DIM, HG_VAL_DIM), F32)
    kp, vp, sp, kd, vd, sd = [], [], [], [], [], []
    for l in range(DEPTH):
        w = (w_in[l], w_out[l], hg_norm[l], ln1_g[l], ln1_b[l], w_router[l], router_bias[l],
             w_exp_gate[l], w_exp_up[l], w_exp_down[l], w_sh_gate[l], w_sh_up[l], w_sh_down[l],
             ln2_g[l], ln2_b[l])
        xp, k1, v1, s1 = trunk_layer(xp, None, None, s0_prompt, lower_bounds[l], *w)
        xs, k2, v2, s2 = trunk_layer(xs, cache_sb_k[l], cache_sb_v[l], state_hgrn[l], lower_bounds[l], *w)
        kp.append(k1); vp.append(v1); sp.append(s1)
        kd.append(k2); vd.append(v2); sd.append(s2)
    return (xp, xs, jnp.stack(kp), jnp.stack(vp), jnp.stack(sp), jnp.stack(kd), jnp.stack(vd), jnp.stack(sd))
```

```python
import functools
import math

import jax
import jax.numpy as jnp
from jax import lax
from jax.experimental import pallas as pl
from jax.experimental.pallas import tpu as pltpu

F32 = jnp.float32
BF16 = jnp.bfloat16

N_HEADS = 8
HEAD_DIM = 128
N_EXPERTS = 64
TOP_K = 8
N_GROUPS = 8
TOPK_GROUPS = 4
GROUP_SIZE = N_EXPERTS // N_GROUPS
ROUTE_SCALE = 2.5
NEG_BIG = -1e9
LB_FLOOR = 1e-30
LN_EPS = 1e-5
RMS_EPS = 1e-6
N_SPLITS = 9

LANES = 128
SUBLANES = 8
VMEM_LIMIT = 48 * 1024 * 1024

HG_CHUNK = 64
HG_SUB = 32
SB_BLOCK = 256
WC_LANES = 128


def _pick_tile(n, pref):
    if n <= pref:
        return n
    for t in range(pref, 7, -1):
        if n % t == 0 and t % SUBLANES == 0:
            return t
    raise ValueError(f"no tile for {n}")


def _params(*sem):
    return pltpu.CompilerParams(dimension_semantics=sem, vmem_limit_bytes=VMEM_LIMIT)


def _layer_norm(x, g, b):
    mu = jnp.mean(x, axis=-1, keepdims=True)
    xc = x - mu
    var = jnp.mean(xc * xc, axis=-1, keepdims=True)
    return xc * lax.rsqrt(var + LN_EPS) * g + b


def _sigmoid(x):
    return 1.0 / (1.0 + jnp.exp(-x))


def _log_sigmoid(x):
    return jnp.minimum(x, 0.0) - jnp.log(1.0 + jnp.exp(-jnp.abs(x)))


def _ln_kernel(x_ref, g_ref, b_ref, y_ref, yb_ref):
    y = _layer_norm(x_ref[...], g_ref[...], b_ref[...])
    y_ref[...] = y
    yb_ref[...] = y.astype(BF16)


def _entry_ln(x, g, b):
    T, D = x.shape
    tm = _pick_tile(T, 512)
    return pl.pallas_call(
        _ln_kernel,
        grid=(T // tm,),
        in_specs=[pl.BlockSpec((tm, D), lambda i: (i, 0)),
                  pl.BlockSpec((1, D), lambda i: (0, 0)),
                  pl.BlockSpec((1, D), lambda i: (0, 0))],
        out_specs=[pl.BlockSpec((tm, D), lambda i: (i, 0)),
                   pl.BlockSpec((tm, D), lambda i: (i, 0))],
        out_shape=[jax.ShapeDtypeStruct((T, D), F32), jax.ShapeDtypeStruct((T, D), BF16)],
        compiler_params=_params("parallel"),
        name="entry_ln",
    )(x, g.reshape(1, D), b.reshape(1, D))


def _in_proj_kernel(x_ref, w_ref, o_ref):
    o_ref[...] = jnp.dot(x_ref[...], w_ref[...], preferred_element_type=F32)


def _in_proj(xb, w_in_b):
    T, D = xb.shape
    tm = _pick_tile(T, 1024)
    return pl.pallas_call(
        _in_proj_kernel,
        grid=(N_SPLITS, T // tm),
        in_specs=[pl.BlockSpec((tm, D), lambda j, i: (i, 0)),
                  pl.BlockSpec((D, D), lambda j, i: (0, j))],
        out_specs=pl.BlockSpec((None, tm, D), lambda j, i: (j, i, 0)),
        out_shape=jax.ShapeDtypeStruct((N_SPLITS, T, D), F32),
        compiler_params=_params("parallel", "parallel"),
        name="in_proj",
    )(xb, w_in_b)


def _sb_block(qs, kb, vb, carry, o, diag):
    tq, tk = qs.shape[0], kb.shape[0]
    z = lax.dot_general(qs, kb, (((1,), (1,)), ((), ())), preferred_element_type=F32)
    lk = _log_sigmoid(-z)
    if diag:
        mask = (lax.broadcasted_iota(jnp.int32, (tq, tk), 1)
                < lax.broadcasted_iota(jnp.int32, (tq, tk), 0))
        lk = jnp.where(mask, lk, 0.0)
    tri = (lax.broadcasted_iota(jnp.int32, (tk, tk), 0)
           >= lax.broadcasted_iota(jnp.int32, (tk, tk), 1)).astype(BF16)
    hi = lk.astype(BF16)
    lo = (lk - hi.astype(F32)).astype(BF16)
    c = (jnp.dot(hi, tri, preferred_element_type=F32)
         + jnp.dot(lo, tri, preferred_element_type=F32))
    w = jnp.exp(z + c + carry)
    if diag:
        w = jnp.where(mask, w, 0.0)
    o = o + jnp.dot(w.astype(BF16), vb, preferred_element_type=F32)
    carry = carry + c[:, 0:1]
    return carry, o


def _sb_prompt_kernel(q_ref, k_ref, v_ref, o_ref, *, blk):
    i = pl.program_id(2)
    qs = (q_ref[...] * (1.0 / math.sqrt(HEAD_DIM))).astype(BF16)
    r0 = pl.multiple_of(i * blk, blk)
    carry = jnp.zeros((blk, 1), F32)
    o = jnp.zeros((blk, HEAD_DIM), F32)
    carry, o = _sb_block(qs, k_ref[pl.ds(r0, blk), :].astype(BF16),
                         v_ref[pl.ds(r0, blk), :].astype(BF16), carry, o, True)

    def body(j, co):
        r = pl.multiple_of((i - 1 - j) * blk, blk)
        return _sb_block(qs, k_ref[pl.ds(r, blk), :].astype(BF16),
                         v_ref[pl.ds(r, blk), :].astype(BF16), co[0], co[1], False)

    carry, o = lax.fori_loop(0, i, body, (carry, o))
    o_ref[...] = o


def _sb_prompt(proj, B, S):
    D = N_HEADS * HEAD_DIM
    blk = _pick_tile(S, SB_BLOCK)
    nq = S // blk
    return pl.pallas_call(
        functools.partial(_sb_prompt_kernel, blk=blk),
        grid=(B, N_HEADS, nq),
        in_specs=[pl.BlockSpec((None, blk, HEAD_DIM), lambda b, h, i: (0, b * nq + i, h)),
                  pl.BlockSpec((None, S, HEAD_DIM), lambda b, h, i: (1, b, h)),
                  pl.BlockSpec((None, S, HEAD_DIM), lambda b, h, i: (2, b, h))],
        out_specs=pl.BlockSpec((blk, HEAD_DIM), lambda b, h, i: (b * nq + i, h)),
        out_shape=jax.ShapeDtypeStruct((B * S, D), F32),
        compiler_params=_params("parallel", "parallel", "arbitrary"),
        name="sb_prompt",
    )(proj, proj, proj)


def _sb_sample_kernel(q_ref, k_ref, v_ref, pk_ref, pv_ref, o_ref, *, pblk):
    T = q_ref.shape[0]
    P = pk_ref.shape[0]
    qs = (q_ref[...] * (1.0 / math.sqrt(HEAD_DIM))).astype(BF16)
    carry = jnp.zeros((T, 1), F32)
    o = jnp.zeros((T, HEAD_DIM), F32)
    carry, o = _sb_block(qs, k_ref[...].astype(BF16), v_ref[...].astype(BF16), carry, o, True)
    for j in range(P // pblk - 1, -1, -1):
        carry, o = _sb_block(qs, pk_ref[j * pblk:(j + 1) * pblk, :].astype(BF16),
                             pv_ref[j * pblk:(j + 1) * pblk, :].astype(BF16), carry, o, False)
    o_ref[...] = o


def _sb_sample(proj, past_k, past_v, row0, B, T):
    D = N_HEADS * HEAD_DIM
    P = past_k.shape[1]
    pblk = _pick_tile(P, SB_BLOCK)
    rb = row0 // T
    return pl.pallas_call(
        functools.partial(_sb_sample_kernel, pblk=pblk),
        grid=(B, N_HEADS),
        in_specs=[pl.BlockSpec((None, T, HEAD_DIM), lambda b, h: (0, rb + b, h)),
                  pl.BlockSpec((None, T, HEAD_DIM), lambda b, h: (1, rb + b, h)),
                  pl.BlockSpec((None, T, HEAD_DIM), lambda b, h: (2, rb + b, h)),
                  pl.BlockSpec((None, P, HEAD_DIM), lambda b, h: (b, 0, h)),
                  pl.BlockSpec((None, P, HEAD_DIM), lambda b, h: (b, 0, h))],
        out_specs=pl.BlockSpec((T, HEAD_DIM), lambda b, h: (b, h)),
        out_shape=jax.ShapeDtypeStruct((B * T, D), F32),
        compiler_params=_params("parallel", "parallel"),
        name="sb_sample",
    )(proj, proj, proj, past_k, past_v)


def _cumsum_rows(x):
    n = x.shape[0]
    row = lax.broadcasted_iota(jnp.int32, x.shape, 0)
    s = 1
    while s < n:
        x = x + jnp.where(row >= s, pltpu.roll(x, s, 0), 0.0)
        s *= 2
    return x


def _hgrn_kernel(*refs, layer, has_s0):
    if has_s0:
        q_ref, f_ref, i_ref, g_ref, lbl_ref, norm_ref, s0_ref, o_ref, sf_ref, st_scr = refs
    else:
        q_ref, f_ref, i_ref, g_ref, lbl_ref, norm_ref, o_ref, sf_ref, st_scr = refs
    c = pl.program_id(1)
    C = q_ref.shape[0]
    n_sub = C // HG_SUB

    @pl.when(c == 0)
    def _():
        for h in range(N_HEADS):
            if has_s0:
                st_scr[h] = s0_ref[h].T
            else:
                st_scr[h] = jnp.zeros((HEAD_DIM, HEAD_DIM), F32)

    lg = lbl_ref[...]
    ex = jnp.exp(lg - jnp.max(lg, axis=0, keepdims=True))
    soft = ex / jnp.sum(ex, axis=0, keepdims=True)
    lb = jnp.zeros((1, lg.shape[1]), F32)
    for m in range(1, layer + 1):
        lb = lb + soft[m:m + 1, :]

    a = f_ref[...]
    x1 = _log_sigmoid(a)
    x2 = _log_sigmoid(-a) + jnp.log(jnp.maximum(lb, LB_FLOOR))
    logf = jnp.maximum(x1, x2) + jnp.log(1.0 + jnp.exp(-jnp.abs(x1 - x2)))
    logf = jnp.minimum(logf, 0.0)
    kk_all = 1.0 - jnp.exp(logf)
    b_all = _cumsum_rows(logf)
    qraw = q_ref[...]
    q_all = qraw * _sigmoid(qraw)
    graw = g_ref[...]
    gate_all = graw * _sigmoid(graw)
    v_all = i_ref[...]

    row = lax.broadcasted_iota(jnp.int32, (C, HEAD_DIM), 0)
    for h in range(N_HEADS):
        hs = slice(h * HEAD_DIM, (h + 1) * HEAD_DIM)
        b = b_all[:, hs]
        q = q_all[:, hs]
        kk = kk_all[:, hs]
        v = v_all[:, hs]
        vb = v.astype(BF16)
        st = st_scr[h]
        qt = (q * jnp.exp(b)).astype(BF16)
        o = lax.dot_general(qt, st.astype(BF16), (((1,), (1,)), ((), ())),
                            preferred_element_type=F32)
        parts = []
        for i in range(n_sub):
            lo_r, hi_r = i * HG_SUB, (i + 1) * HG_SUB
            if i == 0:
                ref_row = jnp.zeros((1, HEAD_DIM), F32)
            else:
                ref_row = b[lo_r - 1:lo_r, :]
            qh = (q[lo_r:hi_r] * jnp.exp(b[lo_r:hi_r] - ref_row)).astype(BF16)
            kh = (kk[0:hi_r] * jnp.exp(ref_row - b[0:hi_r])).astype(BF16)
            sc = lax.dot_general(qh, kh, (((1,), (1,)), ((), ())),
                                 preferred_element_type=F32)
            tpos = lax.broadcasted_iota(jnp.int32, (HG_SUB, hi_r), 0) + lo_r
            spos = lax.broadcasted_iota(jnp.int32, (HG_SUB, hi_r), 1)
            sc = jnp.where(spos <= tpos, sc, 0.0)
            parts.append(jnp.dot(sc.astype(BF16), vb[0:hi_r], preferred_element_type=F32))
        o = o + jnp.concatenate(parts, axis=0)
        b_last = b[C - 1:C, :]
        ke = (kk * jnp.exp(b_last - b)).astype(BF16)
        upd = lax.dot_general(vb, ke, (((0,), (0,)), ((), ())), preferred_element_type=F32)
        st_scr[h] = st * jnp.exp(b_last) + upd
        ms = jnp.mean(o * o, axis=-1, keepdims=True)
        o_ref[:, hs] = o * lax.rsqrt(ms + RMS_EPS) * norm_ref[...] * gate_all[:, hs]

    @pl.when(c == pl.num_programs(1) - 1)
    def _():
        for h in range(N_HEADS):
            sf_ref[h] = st_scr[h].T


def _hgrn(proj, lb_logits, hg_norm, s0, layer, row0, B, T):
    D = N_HEADS * HEAD_DIM
    C = HG_CHUNK
    assert T % C == 0 and row0 % C == 0
    nc = T // C
    rb = row0 // C
    depth = lb_logits.shape[0]

    def act_spec(split):
        return pl.BlockSpec((None, C, D), lambda b, c: (split, rb + b * nc + c, 0))

    in_specs = [act_spec(3), act_spec(4), act_spec(5), act_spec(6),
                pl.BlockSpec((depth, D), lambda b, c: (0, 0)),
                pl.BlockSpec((1, HEAD_DIM), lambda b, c: (0, 0))]
    args = [proj, proj, proj, proj, lb_logits, hg_norm.reshape(1, HEAD_DIM)]
    if s0 is not None:
        in_specs.append(pl.BlockSpec((None, N_HEADS, HEAD_DIM, HEAD_DIM), lambda b, c: (b, 0, 0, 0)))
        args.append(s0)
    return pl.pallas_call(
        functools.partial(_hgrn_kernel, layer=layer, has_s0=s0 is not None),
        grid=(B, nc),
        in_specs=in_specs,
        out_specs=[pl.BlockSpec((C, D), lambda b, c: (b * nc + c, 0)),
                   pl.BlockSpec((None, N_HEADS, HEAD_DIM, HEAD_DIM), lambda b, c: (b, 0, 0, 0))],
        out_shape=[jax.ShapeDtypeStruct((B * T, D), F32),
                   jax.ShapeDtypeStruct((B, N_HEADS, HEAD_DIM, HEAD_DIM), F32)],
        scratch_shapes=[pltpu.VMEM((N_HEADS, HEAD_DIM, HEAD_DIM), F32)],
        compiler_params=_params("parallel", "arbitrary"),
        name="hgrn2",
    )(*args)


def _merge_out_kernel(ga_ref, gb_ref, oa_ref, ob_ref, x_ref, w_ref, g_ref, b_ref, y_ref, yb_ref, *, alpha):
    merged = _sigmoid(ga_ref[...]) * oa_ref[...] + _sigmoid(gb_ref[...]) * ob_ref[...]
    r = alpha * x_ref[...] + jnp.dot(merged.astype(BF16), w_ref[...], preferred_element_type=F32)
    y = _layer_norm(r, g_ref[...], b_ref[...])
    y_ref[...] = y
    yb_ref[...] = y.astype(BF16)


def _merge_out(proj, o_sb, o_hg, x, w_out_b, g, b, alpha):
    T, D = x.shape
    tm = _pick_tile(T, 256)
    row = pl.BlockSpec((tm, D), lambda i: (i, 0))
    vec = pl.BlockSpec((1, D), lambda i: (0, 0))
    return pl.pallas_call(
        functools.partial(_merge_out_kernel, alpha=alpha),
        grid=(T // tm,),
        in_specs=[pl.BlockSpec((None, tm, D), lambda i: (7, i, 0)),
                  pl.BlockSpec((None, tm, D), lambda i: (8, i, 0)),
                  row, row, row,
                  pl.BlockSpec((D, D), lambda i: (0, 0)), vec, vec],
        out_specs=[row, row],
        out_shape=[jax.ShapeDtypeStruct((T, D), F32), jax.ShapeDtypeStruct((T, D), BF16)],
        compiler_params=_params("parallel"),
        name="merge_out",
    )(proj, proj, o_sb, o_hg, x, w_out_b, g.reshape(1, D), b.reshape(1, D))


def _router_kernel(x_ref, wr_ref, bias_ref, wc_ref):
    tm = x_ref.shape[0]
    logits = lax.dot_general(wr_ref[...], x_ref[...], (((1,), (1,)), ((), ())),
                             precision=lax.Precision.HIGHEST, preferred_element_type=F32)
    s = _sigmoid(logits)
    sb = s + bias_ref[...]
    sub = lax.broadcasted_iota(jnp.int32, (GROUP_SIZE, tm), 0)
    gs = []
    for g in range(N_GROUPS):
        blk = sb[g * GROUP_SIZE:(g + 1) * GROUP_SIZE, :]
        m1 = jnp.max(blk, axis=0, keepdims=True)
        first = jnp.min(jnp.where(blk == m1, sub, GROUP_SIZE), axis=0, keepdims=True)
        m2 = jnp.max(jnp.where(sub == first, -jnp.inf, blk), axis=0, keepdims=True)
        gs.append(m1 + m2)
    masked = []
    for g in range(N_GROUPS):
        beat = jnp.zeros((1, tm), jnp.int32)
        for g2 in range(N_GROUPS):
            if g2 == g:
                continue
            if g2 < g:
                beat = beat + (gs[g2] >= gs[g]).astype(jnp.int32)
            else:
                beat = beat + (gs[g2] > gs[g]).astype(jnp.int32)
        keep = beat < TOPK_GROUPS
        masked.append(jnp.where(keep, sb[g * GROUP_SIZE:(g + 1) * GROUP_SIZE, :], NEG_BIG))
    m = jnp.concatenate(masked, axis=0)
    erow = lax.broadcasted_iota(jnp.int32, (N_EXPERTS, tm), 0)
    sel = jnp.zeros((N_EXPERTS, tm), jnp.bool_)
    for _ in range(TOP_K):
        mx = jnp.max(m, axis=0, keepdims=True)
        first = jnp.min(jnp.where(m == mx, erow, N_EXPERTS), axis=0, keepdims=True)
        hit = erow == first
        sel = jnp.logical_or(sel, hit)
        m = jnp.where(hit, -jnp.inf, m)
    w = jnp.where(sel, s, 0.0)
    w = w / jnp.sum(w, axis=0, keepdims=True) * ROUTE_SCALE
    pad_rows = lax.broadcasted_iota(jnp.int32, (WC_LANES - N_EXPERTS, tm), 0)
    tail = jnp.where(pad_rows == 0, 1.0, 0.0).astype(F32)
    wc_ref[...] = jnp.concatenate([w, tail], axis=0).T


def _router(x, w_router, bias):
    T, D = x.shape
    tm = _pick_tile(T, 256)
    return pl.pallas_call(
        _router_kernel,
        grid=(T // tm,),
        in_specs=[pl.BlockSpec((tm, D), lambda i: (i, 0)),
                  pl.BlockSpec((N_EXPERTS, D), lambda i: (0, 0)),
                  pl.BlockSpec((N_EXPERTS, 1), lambda i: (0, 0))],
        out_specs=pl.BlockSpec((tm, WC_LANES), lambda i: (i, 0)),
        out_shape=jax.ShapeDtypeStruct((T, WC_LANES), F32),
        compiler_params=_params("parallel"),
        name="router",
    )(x, w_router.T, bias.reshape(N_EXPERTS, 1))


def _moe_kernel(xb_ref, x_ref, wc_ref, wg_ref, wu_ref, wd_ref, g_ref, b_ref, y_ref, yb_ref, acc_ref, *, alpha):
    e = pl.program_id(1)

    @pl.when(e == 0)
    def _():
        acc_ref[...] = jnp.zeros_like(acc_ref)

    xb = xb_ref[...]
    lane = lax.broadcasted_iota(jnp.int32, wc_ref.shape, 1)
    col = jnp.sum(jnp.where(lane == e, wc_ref[...], 0.0), axis=1, keepdims=True)
    hg = jnp.dot(xb, wg_ref[...], preferred_element_type=F32)
    hu = jnp.dot(xb, wu_ref[...], preferred_element_type=F32)
    h = hg * _sigmoid(hg) * hu * col
    acc_ref[...] += jnp.dot(h.astype(BF16), wd_ref[...], preferred_element_type=F32)

    @pl.when(e == pl.num_programs(1) - 1)
    def _():
        y = _layer_norm(alpha * x_ref[...] + acc_ref[...], g_ref[...], b_ref[...])
        y_ref[...] = y
        yb_ref[...] = y.astype(BF16)


def _moe(xb, x, wc, wg, wu, wd, g, b, alpha):
    T, D = x.shape
    n_e, _, F = wg.shape
    tm = _pick_tile(T, 1024)
    row = pl.BlockSpec((tm, D), lambda i, e: (i, 0))
    vec = pl.BlockSpec((1, D), lambda i, e: (0, 0))
    return pl.pallas_call(
        functools.partial(_moe_kernel, alpha=alpha),
        grid=(T // tm, n_e),
        in_specs=[row, row,
                  pl.BlockSpec((tm, WC_LANES), lambda i, e: (i, 0)),
                  pl.BlockSpec((None, D, F), lambda i, e: (e, 0, 0)),
                  pl.BlockSpec((None, D, F), lambda i, e: (e, 0, 0)),
                  pl.BlockSpec((None, F, D), lambda i, e: (e, 0, 0)),
                  vec, vec],
        out_specs=[row, row],
        out_shape=[jax.ShapeDtypeStruct((T, D), F32), jax.ShapeDtypeStruct((T, D), BF16)],
        scratch_shapes=[pltpu.VMEM((tm, D), F32)],
        compiler_params=_params("parallel", "arbitrary"),
        name="moe",
    )(xb, x, wc, wg, wu, wd, g.reshape(1, D), b.reshape(1, D))


def kernel(x_prompt, x_sample, cache_sb_k, cache_sb_v, state_hgrn, ln_in_g, ln_in_b, w_in, w_out, hg_norm, hg_lb_logits, ln1_g, ln1_b, w_router, router_bias, w_exp_gate, w_exp_up, w_exp_down, w_sh_gate, w_sh_up, w_sh_down, ln2_g, ln2_b):
    B, S, D = x_prompt.shape
    Bs, Ts, _ = x_sample.shape
    depth = w_in.shape[0]
    P = cache_sb_k.shape[2]
    n_p = B * S
    alpha = (2 * depth) ** 0.25

    x_all = jnp.concatenate([x_prompt.reshape(n_p, D), x_sample.reshape(Bs * Ts, D)], axis=0)
    x, xb = _entry_ln(x_all, ln_in_g, ln_in_b)

    kp, vp, sp, kd, vd, sd = [], [], [], [], [], []
    for l in range(depth):
        proj = _in_proj(xb, w_in[l].astype(BF16))
        o_sb_p = _sb_prompt(proj, B, S)
        o_sb_s = _sb_sample(proj, cache_sb_k[l].reshape(Bs, P, D), cache_sb_v[l].reshape(Bs, P, D),
                            n_p, Bs, Ts)
        o_hg_p, s_p = _hgrn(proj, hg_lb_logits, hg_norm[l], None, l, 0, B, S)
        o_hg_s, s_s = _hgrn(proj, hg_lb_logits, hg_norm[l], state_hgrn[l], l, n_p, Bs, Ts)
        o_sb = jnp.concatenate([o_sb_p, o_sb_s], axis=0)
        o_hg = jnp.concatenate([o_hg_p, o_hg_s], axis=0)
        x, xb = _merge_out(proj, o_sb, o_hg, x, w_out[l].astype(BF16), ln1_g[l], ln1_b[l], alpha)
        wc = _router(x, w_router[l], router_bias[l])
        wg = jnp.concatenate([w_exp_gate[l], w_sh_gate[l][None]], axis=0).astype(BF16)
        wu = jnp.concatenate([w_exp_up[l], w_sh_up[l][None]], axis=0).astype(BF16)
        wd = jnp.concatenate([w_exp_down[l], w_sh_down[l][None]], axis=0).astype(BF16)
        x, xb = _moe(xb, x, wc, wg, wu, wd, ln2_g[l], ln2_b[l], alpha)
        kp.append(proj[1, :n_p].reshape(B, S, N_HEADS, HEAD_DIM))
        vp.append(proj[2, :n_p].reshape(B, S, N_HEADS, HEAD_DIM))
        kd.append(proj[1, n_p:].reshape(Bs, Ts, N_HEADS, HEAD_DIM))
        vd.append(proj[2, n_p:].reshape(Bs, Ts, N_HEADS, HEAD_DIM))
        sp.append(s_p)
        sd.append(s_s)
    return (x[:n_p].reshape(B, S, D), x[n_p:].reshape(Bs, Ts, D),
            jnp.stack(kp), jnp.stack(vp), jnp.stack(sp), jnp.stack(kd), jnp.stack(vd), jnp.stack(sd))
```

```python
import functools
import math

import jax
import jax.numpy as jnp
from jax import lax
from jax.experimental import pallas as pl
from jax.experimental.pallas import tpu as pltpu

F32 = jnp.float32
BF16 = jnp.bfloat16

N_HEADS = 8
HEAD_DIM = 128
N_EXPERTS = 64
TOP_K = 8
N_GROUPS = 8
TOPK_GROUPS = 4
GROUP_SIZE = N_EXPERTS // N_GROUPS
ROUTE_SCALE = 2.5
NEG_BIG = -1e9
LB_FLOOR = 1e-30
LN_EPS = 1e-5
RMS_EPS = 1e-6
LOG2E = 1.4426950408889634
N_SPLITS = 9

LANES = 128
SUBLANES = 8
VMEM_LIMIT = 48 * 1024 * 1024

HG_CHUNK = 64
HG_SUB = 32
SB_BLOCK = 256
SB_GROUP = 4
WC_LANES = 128


def _pick_tile(n, pref):
    if n <= pref:
        return n
    for t in range(pref, 7, -1):
        if n % t == 0 and t % SUBLANES == 0:
            return t
    raise ValueError(f"no tile for {n}")


def _params(*sem):
    return pltpu.CompilerParams(dimension_semantics=sem, vmem_limit_bytes=VMEM_LIMIT)


def _layer_norm(x, g, b):
    mu = jnp.mean(x, axis=-1, keepdims=True)
    xc = x - mu
    var = jnp.mean(xc * xc, axis=-1, keepdims=True)
    return xc * lax.rsqrt(var + LN_EPS) * g + b


def _sigmoid(x):
    return 1.0 / (1.0 + jnp.exp(-x))


def _log_sigmoid(x):
    return jnp.minimum(x, 0.0) - jnp.log(1.0 + jnp.exp(-jnp.abs(x)))


def _ln_kernel(x_ref, g_ref, b_ref, y_ref, yb_ref):
    y = _layer_norm(x_ref[...], g_ref[...], b_ref[...])
    y_ref[...] = y
    yb_ref[...] = y.astype(BF16)


def _entry_ln(x, g, b):
    T, D = x.shape
    tm = _pick_tile(T, 512)
    return pl.pallas_call(
        _ln_kernel,
        grid=(T // tm,),
        in_specs=[pl.BlockSpec((tm, D), lambda i: (i, 0)),
                  pl.BlockSpec((1, D), lambda i: (0, 0)),
                  pl.BlockSpec((1, D), lambda i: (0, 0))],
        out_specs=[pl.BlockSpec((tm, D), lambda i: (i, 0)),
                   pl.BlockSpec((tm, D), lambda i: (i, 0))],
        out_shape=[jax.ShapeDtypeStruct((T, D), F32), jax.ShapeDtypeStruct((T, D), BF16)],
        compiler_params=_params("parallel"),
        name="entry_ln",
    )(x, g.reshape(1, D), b.reshape(1, D))


def _in_proj_kernel(x_ref, w_ref, o_ref):
    o_ref[...] = jnp.dot(x_ref[...], w_ref[...], preferred_element_type=F32)


def _in_proj(xb, w_in_b):
    T, D = xb.shape
    tm = _pick_tile(T, 1024)
    return pl.pallas_call(
        _in_proj_kernel,
        grid=(N_SPLITS, T // tm),
        in_specs=[pl.BlockSpec((tm, D), lambda j, i: (i, 0)),
                  pl.BlockSpec((D, D), lambda j, i: (0, j))],
        out_specs=pl.BlockSpec((None, tm, D), lambda j, i: (j, i, 0)),
        out_shape=jax.ShapeDtypeStruct((N_SPLITS, T, D), F32),
        compiler_params=_params("parallel", "parallel"),
        name="in_proj",
    )(xb, w_in_b)


def _neg_tri2(tk):
    r = lax.broadcasted_iota(jnp.int32, (2 * tk, tk), 0)
    c = lax.broadcasted_iota(jnp.int32, (2 * tk, tk), 1)
    r = jnp.where(r >= tk, r - tk, r)
    return jnp.where(r >= c, -1.0, 0.0).astype(BF16)


def _strict_lower(tq, tk):
    return (lax.broadcasted_iota(jnp.int32, (tq, tk), 1)
            < lax.broadcasted_iota(jnp.int32, (tq, tk), 0))


def _sb_z(qs, kb):
    return lax.dot_general(qs, kb, (((1,), (1,)), ((), ())), preferred_element_type=F32)


def _sb_cumsum_lhs(z, tk, mask):
    sp = jnp.maximum(z, 0.0) + jnp.log(1.0 + jnp.exp2(jnp.abs(z) * (-LOG2E)))
    if mask is not None:
        sp = jnp.where(mask, sp, 0.0)
    hi = sp.astype(BF16)
    lo = (sp - hi.astype(F32)).astype(BF16)
    n = z.shape[1] // tk
    parts = [jnp.concatenate([hi[:, a * tk:(a + 1) * tk], lo[:, a * tk:(a + 1) * tk]], axis=1)
             for a in range(n)]
    return parts[0] if n == 1 else jnp.concatenate(parts, axis=0)


def _sb_weights(z, c, carry, tk, mask):
    tq = z.shape[0]
    n = z.shape[1] // tk
    ws = [None] * n
    for a in range(n - 1, -1, -1):
        ca = c[a * tq:(a + 1) * tq, :]
        w = jnp.exp2((z[:, a * tk:(a + 1) * tk] + ca + carry) * LOG2E)
        if mask is not None:
            w = jnp.where(mask, w, 0.0)
        ws[a] = w.astype(BF16)
        carry = carry + ca[:, 0:1]
    return (ws[0] if n == 1 else jnp.concatenate(ws, axis=1)), carry


def _sb_prompt_kernel(q_ref, k_ref, v_ref, o_ref, kb_scr, vb_scr, tri_scr, *, blk):
    i = pl.program_id(2)

    @pl.when(i == 0)
    def _():
        kb_scr[...] = k_ref[...].astype(BF16)
        vb_scr[...] = v_ref[...].astype(BF16)
        tri_scr[...] = _neg_tri2(blk)

    qs = (q_ref[...] * (1.0 / math.sqrt(HEAD_DIM))).astype(BF16)

    def sweep(j_lo, n, co, mask=None):
        carry, o = co
        rows = [pl.ds(pl.multiple_of((j_lo + a) * blk, blk), blk) for a in range(n)]
        zs = [_sb_z(qs, kb_scr[r, :]) for r in rows]
        cs = [jnp.dot(_sb_cumsum_lhs(z, blk, mask), tri_scr[...], preferred_element_type=F32)
              for z in zs]
        for a in range(n - 1, -1, -1):
            w, carry = _sb_weights(zs[a], cs[a], carry, blk, mask)
            o = o + jnp.dot(w, vb_scr[rows[a], :], preferred_element_type=F32)
        return carry, o

    co = (jnp.zeros((blk, 1), F32), jnp.zeros((blk, HEAD_DIM), F32))
    co = sweep(i, 1, co, _strict_lower(blk, blk))
    one = i % 2
    two = (i // 2) % 2
    co = lax.cond(one == 1, lambda co: sweep(i - 1, 1, co), lambda co: co, co)
    co = lax.cond(two == 1, lambda co: sweep(i - one - 2, 2, co), lambda co: co, co)
    j0 = i - one - 2 * two
    co = lax.fori_loop(0, i // SB_GROUP, lambda p, co: sweep(j0 - SB_GROUP * (p + 1), SB_GROUP, co), co)
    o_ref[...] = co[1]


def _sb_prompt(proj, B, S):
    D = N_HEADS * HEAD_DIM
    t_all = proj.shape[1]
    blk = _pick_tile(S, SB_BLOCK)
    nq = S // blk
    return pl.pallas_call(
        functools.partial(_sb_prompt_kernel, blk=blk),
        grid=(B, N_HEADS, nq),
        in_specs=[pl.BlockSpec((None, blk, HEAD_DIM), lambda b, h, i: (0, b * nq + i, h)),
                  pl.BlockSpec((None, S, HEAD_DIM), lambda b, h, i: (1, b, h)),
                  pl.BlockSpec((None, S, HEAD_DIM), lambda b, h, i: (2, b, h))],
        out_specs=pl.BlockSpec((blk, HEAD_DIM), lambda b, h, i: (b * nq + i, h)),
        out_shape=jax.ShapeDtypeStruct((t_all, D), F32),
        scratch_shapes=[pltpu.VMEM((S, HEAD_DIM), BF16), pltpu.VMEM((S, HEAD_DIM), BF16),
                        pltpu.VMEM((2 * blk, blk), BF16)],
        compiler_params=_params("parallel", "parallel", "arbitrary"),
        name="sb_prompt",
    )(proj, proj, proj)


def _sb_sample_kernel(q_ref, k_ref, v_ref, pk_ref, pv_ref, o_in_ref, o_ref, *, pblk):
    del o_in_ref
    T = q_ref.shape[0]
    P = pk_ref.shape[0] // N_HEADS

    def head_rows(ref, h):
        return ref[pl.ds(h, P, stride=N_HEADS), :].astype(BF16)

    heads = range(N_HEADS)
    cols = [slice(h * HEAD_DIM, (h + 1) * HEAD_DIM) for h in heads]
    mask = _strict_lower(T, T)
    qs = [(q_ref[:, cols[h]] * (1.0 / math.sqrt(HEAD_DIM))).astype(BF16) for h in heads]
    z_new = [_sb_z(qs[h], k_ref[:, cols[h]].astype(BF16)) for h in heads]
    z_past = [_sb_z(qs[h], head_rows(pk_ref, h)) for h in heads]
    lhs_new = jnp.concatenate([_sb_cumsum_lhs(z_new[h], T, mask) for h in heads], axis=0)
    lhs_past = jnp.concatenate([_sb_cumsum_lhs(z_past[h], pblk, None) for h in heads], axis=0)
    c_new = jnp.dot(lhs_new, _neg_tri2(T), preferred_element_type=F32)
    c_past = jnp.dot(lhs_past, _neg_tri2(pblk), preferred_element_type=F32)
    rows_past = (P // pblk) * T
    for h in heads:
        w_new, carry = _sb_weights(z_new[h], c_new[h * T:(h + 1) * T], jnp.zeros((T, 1), F32), T, mask)
        w_past, _ = _sb_weights(z_past[h], c_past[h * rows_past:(h + 1) * rows_past], carry, pblk, None)
        o_ref[:, cols[h]] = (jnp.dot(w_new, v_ref[:, cols[h]].astype(BF16), preferred_element_type=F32)
                             + jnp.dot(w_past, head_rows(pv_ref, h), preferred_element_type=F32))


def _sb_sample(proj, cache_k, cache_v, o_sb, layer, row0, B, T):
    D = N_HEADS * HEAD_DIM
    depth, _, P = cache_k.shape[:3]
    pblk = _pick_tile(P, SB_BLOCK)
    rb = row0 // T
    cache_k = cache_k.reshape(depth, B, P * N_HEADS, HEAD_DIM)
    cache_v = cache_v.reshape(depth, B, P * N_HEADS, HEAD_DIM)
    cache_spec = pl.BlockSpec((None, None, P * N_HEADS, HEAD_DIM), lambda b: (layer, b, 0, 0))
    return pl.pallas_call(
        functools.partial(_sb_sample_kernel, pblk=pblk),
        grid=(B,),
        in_specs=[pl.BlockSpec((None, T, D), lambda b: (0, rb + b, 0)),
                  pl.BlockSpec((None, T, D), lambda b: (1, rb + b, 0)),
                  pl.BlockSpec((None, T, D), lambda b: (2, rb + b, 0)),
                  cache_spec, cache_spec,
                  pl.BlockSpec(memory_space=pl.ANY)],
        out_specs=pl.BlockSpec((T, D), lambda b: (rb + b, 0)),
        out_shape=jax.ShapeDtypeStruct(o_sb.shape, F32),
        input_output_aliases={5: 0},
        compiler_params=_params("parallel"),
        name="sb_sample",
    )(proj, proj, proj, cache_k, cache_v, o_sb)


def _cumsum_rows(x):
    n = x.shape[0]
    row = lax.broadcasted_iota(jnp.int32, x.shape, 0)
    s = 1
    while s < n:
        x = x + jnp.where(row >= s, pltpu.roll(x, s, 0), 0.0)
        s *= 2
    return x


def _hgrn_kernel(*refs, layer, has_s0):
    if has_s0:
        q_ref, f_ref, i_ref, g_ref, lbl_ref, norm_ref, s0_ref, _, o_ref, sf_ref, st_scr = refs
    else:
        q_ref, f_ref, i_ref, g_ref, lbl_ref, norm_ref, o_ref, sf_ref, st_scr = refs
    c = pl.program_id(1)
    C = q_ref.shape[0]
    n_sub = C // HG_SUB

    @pl.when(c == 0)
    def _():
        for h in range(N_HEADS):
            if has_s0:
                st_scr[h] = s0_ref[h].T
            else:
                st_scr[h] = jnp.zeros((HEAD_DIM, HEAD_DIM), F32)

    lg = lbl_ref[...]
    ex = jnp.exp(lg - jnp.max(lg, axis=0, keepdims=True))
    soft = ex / jnp.sum(ex, axis=0, keepdims=True)
    lb = jnp.zeros((1, lg.shape[1]), F32)
    for m in range(1, layer + 1):
        lb = lb + soft[m:m + 1, :]

    a = f_ref[...]
    x1 = _log_sigmoid(a)
    x2 = _log_sigmoid(-a) + jnp.log(jnp.maximum(lb, LB_FLOOR))
    logf = jnp.maximum(x1, x2) + jnp.log(1.0 + jnp.exp(-jnp.abs(x1 - x2)))
    logf = jnp.minimum(logf, 0.0)
    kk_all = 1.0 - jnp.exp(logf)
    b_all = _cumsum_rows(logf)
    qraw = q_ref[...]
    q_all = qraw * _sigmoid(qraw)
    graw = g_ref[...]
    gate_all = graw * _sigmoid(graw)
    v_all = i_ref[...]

    row = lax.broadcasted_iota(jnp.int32, (C, HEAD_DIM), 0)
    for h in range(N_HEADS):
        hs = slice(h * HEAD_DIM, (h + 1) * HEAD_DIM)
        b = b_all[:, hs]
        q = q_all[:, hs]
        kk = kk_all[:, hs]
        v = v_all[:, hs]
        vb = v.astype(BF16)
        st = st_scr[h]
        qt = (q * jnp.exp(b)).astype(BF16)
        o = lax.dot_general(qt, st.astype(BF16), (((1,), (1,)), ((), ())),
                            preferred_element_type=F32)
        parts = []
        for i in range(n_sub):
            lo_r, hi_r = i * HG_SUB, (i + 1) * HG_SUB
            if i == 0:
                ref_row = jnp.zeros((1, HEAD_DIM), F32)
            else:
                ref_row = b[lo_r - 1:lo_r, :]
            qh = (q[lo_r:hi_r] * jnp.exp(b[lo_r:hi_r] - ref_row)).astype(BF16)
            kh = (kk[0:hi_r] * jnp.exp(ref_row - b[0:hi_r])).astype(BF16)
            sc = lax.dot_general(qh, kh, (((1,), (1,)), ((), ())),
                                 preferred_element_type=F32)
            tpos = lax.broadcasted_iota(jnp.int32, (HG_SUB, hi_r), 0) + lo_r
            spos = lax.broadcasted_iota(jnp.int32, (HG_SUB, hi_r), 1)
            sc = jnp.where(spos <= tpos, sc, 0.0)
            parts.append(jnp.dot(sc.astype(BF16), vb[0:hi_r], preferred_element_type=F32))
        o = o + jnp.concatenate(parts, axis=0)
        b_last = b[C - 1:C, :]
        ke = (kk * jnp.exp(b_last - b)).astype(BF16)
        upd = lax.dot_general(vb, ke, (((0,), (0,)), ((), ())), preferred_element_type=F32)
        st_scr[h] = st * jnp.exp(b_last) + upd
        ms = jnp.mean(o * o, axis=-1, keepdims=True)
        o_ref[:, hs] = o * lax.rsqrt(ms + RMS_EPS) * norm_ref[...] * gate_all[:, hs]

    @pl.when(c == pl.num_programs(1) - 1)
    def _():
        for h in range(N_HEADS):
            sf_ref[h] = st_scr[h].T


def _hgrn(proj, lb_logits, hg_norm, s0, o_prev, layer, row0, B, T):
    D = N_HEADS * HEAD_DIM
    C = HG_CHUNK
    assert T % C == 0 and row0 % C == 0
    nc = T // C
    rb = row0 // C
    depth = lb_logits.shape[0]
    t_all = proj.shape[1]

    def act_spec(split):
        return pl.BlockSpec((None, C, D), lambda b, c: (split, rb + b * nc + c, 0))

    in_specs = [act_spec(3), act_spec(4), act_spec(5), act_spec(6),
                pl.BlockSpec((depth, D), lambda b, c: (0, 0)),
                pl.BlockSpec((1, HEAD_DIM), lambda b, c: (0, 0))]
    args = [proj, proj, proj, proj, lb_logits, hg_norm.reshape(1, HEAD_DIM)]
    if s0 is not None:
        in_specs.append(pl.BlockSpec((None, N_HEADS, HEAD_DIM, HEAD_DIM), lambda b, c: (b, 0, 0, 0)))
        in_specs.append(pl.BlockSpec(memory_space=pl.ANY))
        args += [s0, o_prev]
    return pl.pallas_call(
        functools.partial(_hgrn_kernel, layer=layer, has_s0=s0 is not None),
        grid=(B, nc),
        in_specs=in_specs,
        out_specs=[pl.BlockSpec((C, D), lambda b, c: (rb + b * nc + c, 0)),
                   pl.BlockSpec((None, N_HEADS, HEAD_DIM, HEAD_DIM), lambda b, c: (b, 0, 0, 0))],
        out_shape=[jax.ShapeDtypeStruct((t_all, D), F32),
                   jax.ShapeDtypeStruct((B, N_HEADS, HEAD_DIM, HEAD_DIM), F32)],
        input_output_aliases={} if s0 is None else {len(args) - 1: 0},
        scratch_shapes=[pltpu.VMEM((N_HEADS, HEAD_DIM, HEAD_DIM), F32)],
        compiler_params=_params("parallel", "arbitrary"),
        name="hgrn2",
    )(*args)


def _merge_out_kernel(ga_ref, gb_ref, oa_ref, ob_ref, x_ref, w_ref, g_ref, b_ref, y_ref, yb_ref, *, alpha):
    merged = _sigmoid(ga_ref[...]) * oa_ref[...] + _sigmoid(gb_ref[...]) * ob_ref[...]
    r = alpha * x_ref[...] + jnp.dot(merged.astype(BF16), w_ref[...], preferred_element_type=F32)
    y = _layer_norm(r, g_ref[...], b_ref[...])
    y_ref[...] = y
    yb_ref[...] = y.astype(BF16)


def _merge_out(proj, o_sb, o_hg, x, w_out_b, g, b, alpha):
    T, D = x.shape
    tm = _pick_tile(T, 256)
    row = pl.BlockSpec((tm, D), lambda i: (i, 0))
    vec = pl.BlockSpec((1, D), lambda i: (0, 0))
    return pl.pallas_call(
        functools.partial(_merge_out_kernel, alpha=alpha),
        grid=(T // tm,),
        in_specs=[pl.BlockSpec((None, tm, D), lambda i: (7, i, 0)),
                  pl.BlockSpec((None, tm, D), lambda i: (8, i, 0)),
                  row, row, row,
                  pl.BlockSpec((D, D), lambda i: (0, 0)), vec, vec],
        out_specs=[row, row],
        out_shape=[jax.ShapeDtypeStruct((T, D), F32), jax.ShapeDtypeStruct((T, D), BF16)],
        compiler_params=_params("parallel"),
        name="merge_out",
    )(proj, proj, o_sb, o_hg, x, w_out_b, g.reshape(1, D), b.reshape(1, D))


def _router_kernel(x_ref, wr_ref, bias_ref, wc_ref):
    tm = x_ref.shape[0]
    logits = lax.dot_general(wr_ref[...], x_ref[...], (((1,), (1,)), ((), ())),
                             precision=lax.Precision.HIGHEST, preferred_element_type=F32)
    s = _sigmoid(logits)
    sb = s + bias_ref[...]
    sub = lax.broadcasted_iota(jnp.int32, (GROUP_SIZE, tm), 0)
    gs = []
    for g in range(N_GROUPS):
        blk = sb[g * GROUP_SIZE:(g + 1) * GROUP_SIZE, :]
        m1 = jnp.max(blk, axis=0, keepdims=True)
        first = jnp.min(jnp.where(blk == m1, sub, GROUP_SIZE), axis=0, keepdims=True)
        m2 = jnp.max(jnp.where(sub == first, -jnp.inf, blk), axis=0, keepdims=True)
        gs.append(m1 + m2)
    masked = []
    for g in range(N_GROUPS):
        beat = jnp.zeros((1, tm), jnp.int32)
        for g2 in range(N_GROUPS):
            if g2 == g:
                continue
            if g2 < g:
                beat = beat + (gs[g2] >= gs[g]).astype(jnp.int32)
            else:
                beat = beat + (gs[g2] > gs[g]).astype(jnp.int32)
        keep = beat < TOPK_GROUPS
        masked.append(jnp.where(keep, sb[g * GROUP_SIZE:(g + 1) * GROUP_SIZE, :], NEG_BIG))
    m = jnp.concatenate(masked, axis=0)
    erow = lax.broadcasted_iota(jnp.int32, (N_EXPERTS, tm), 0)
    sel = jnp.zeros((N_EXPERTS, tm), jnp.bool_)
    for _ in range(TOP_K):
        mx = jnp.max(m, axis=0, keepdims=True)
        first = jnp.min(jnp.where(m == mx, erow, N_EXPERTS), axis=0, keepdims=True)
        hit = erow == first
        sel = jnp.logical_or(sel, hit)
        m = jnp.where(hit, -jnp.inf, m)
    w = jnp.where(sel, s, 0.0)
    w = w / jnp.sum(w, axis=0, keepdims=True) * ROUTE_SCALE
    pad_rows = lax.broadcasted_iota(jnp.int32, (WC_LANES - N_EXPERTS, tm), 0)
    tail = jnp.where(pad_rows == 0, 1.0, 0.0).astype(F32)
    wc_ref[...] = jnp.concatenate([w, tail], axis=0).T


def _router(x, w_router, bias):
    T, D = x.shape
    tm = _pick_tile(T, 256)
    return pl.pallas_call(
        _router_kernel,
        grid=(T // tm,),
        in_specs=[pl.BlockSpec((tm, D), lambda i: (i, 0)),
                  pl.BlockSpec((N_EXPERTS, D), lambda i: (0, 0)),
                  pl.BlockSpec((N_EXPERTS, 1), lambda i: (0, 0))],
        out_specs=pl.BlockSpec((tm, WC_LANES), lambda i: (i, 0)),
        out_shape=jax.ShapeDtypeStruct((T, WC_LANES), F32),
        compiler_params=_params("parallel"),
        name="router",
    )(x, w_router.T, bias.reshape(N_EXPERTS, 1))


def _moe_kernel(xb_ref, x_ref, wc_ref, wg_ref, wu_ref, wd_ref, sg_ref, su_ref, sd_ref, g_ref, b_ref,
                y_ref, yb_ref, acc_ref, *, alpha, n_routed):
    e = pl.program_id(1)

    @pl.when(e == 0)
    def _():
        acc_ref[...] = jnp.zeros_like(acc_ref)

    def expert(wg, wu, wd):
        xb = xb_ref[...]
        lane = lax.broadcasted_iota(jnp.int32, wc_ref.shape, 1)
        col = jnp.sum(jnp.where(lane == e, wc_ref[...], 0.0), axis=1, keepdims=True)
        hg = jnp.dot(xb, wg[...], preferred_element_type=F32)
        hu = jnp.dot(xb, wu[...], preferred_element_type=F32)
        h = hg * _sigmoid(hg) * hu * col
        acc_ref[...] += jnp.dot(h.astype(BF16), wd[...], preferred_element_type=F32)

    @pl.when(e < n_routed)
    def _():
        expert(wg_ref, wu_ref, wd_ref)

    @pl.when(e == n_routed)
    def _():
        expert(sg_ref, su_ref, sd_ref)

    @pl.when(e == pl.num_programs(1) - 1)
    def _():
        y = _layer_norm(alpha * x_ref[...] + acc_ref[...], g_ref[...], b_ref[...])
        y_ref[...] = y
        yb_ref[...] = y.astype(BF16)


def _moe(xb, x, wc, wg, wu, wd, sg, su, sd, g, b, alpha):
    T, D = x.shape
    n_e, _, F = wg.shape
    tm = _pick_tile(T, 1024)
    row = pl.BlockSpec((tm, D), lambda i, e: (i, 0))
    vec = pl.BlockSpec((1, D), lambda i, e: (0, 0))
    last = n_e - 1
    return pl.pallas_call(
        functools.partial(_moe_kernel, alpha=alpha, n_routed=n_e),
        grid=(T // tm, n_e + 1),
        in_specs=[row, row,
                  pl.BlockSpec((tm, WC_LANES), lambda i, e: (i, 0)),
                  pl.BlockSpec((None, D, F), lambda i, e: (jnp.minimum(e, last), 0, 0)),
                  pl.BlockSpec((None, D, F), lambda i, e: (jnp.minimum(e, last), 0, 0)),
                  pl.BlockSpec((None, F, D), lambda i, e: (jnp.minimum(e, last), 0, 0)),
                  pl.BlockSpec((D, F), lambda i, e: (0, 0)),
                  pl.BlockSpec((D, F), lambda i, e: (0, 0)),
                  pl.BlockSpec((F, D), lambda i, e: (0, 0)),
                  vec, vec],
        out_specs=[row, row],
        out_shape=[jax.ShapeDtypeStruct((T, D), F32), jax.ShapeDtypeStruct((T, D), BF16)],
        scratch_shapes=[pltpu.VMEM((tm, D), F32)],
        compiler_params=_params("parallel", "arbitrary"),
        name="moe",
    )(xb, x, wc, wg, wu, wd, sg, su, sd, g.reshape(1, D), b.reshape(1, D))


def kernel(x_prompt, x_sample, cache_sb_k, cache_sb_v, state_hgrn, ln_in_g, ln_in_b, w_in, w_out, hg_norm, hg_lb_logits, ln1_g, ln1_b, w_router, router_bias, w_exp_gate, w_exp_up, w_exp_down, w_sh_gate, w_sh_up, w_sh_down, ln2_g, ln2_b):
    B, S, D = x_prompt.shape
    Bs, Ts, _ = x_sample.shape
    depth = w_in.shape[0]
    P = cache_sb_k.shape[2]
    n_p = B * S
    alpha = (2 * depth) ** 0.25

    x_all = jnp.concatenate([x_prompt.reshape(n_p, D), x_sample.reshape(Bs * Ts, D)], axis=0)
    x, xb = _entry_ln(x_all, ln_in_g, ln_in_b)

    kp, vp, sp, kd, vd, sd = [], [], [], [], [], []
    for l in range(depth):
        proj = _in_proj(xb, w_in[l].astype(BF16))
        o_sb = _sb_prompt(proj, B, S)
        o_sb = _sb_sample(proj, cache_sb_k, cache_sb_v, o_sb, l, n_p, Bs, Ts)
        o_hg, s_p = _hgrn(proj, hg_lb_logits, hg_norm[l], None, None, l, 0, B, S)
        o_hg, s_s = _hgrn(proj, hg_lb_logits, hg_norm[l], state_hgrn[l], o_hg, l, n_p, Bs, Ts)
        x, xb = _merge_out(proj, o_sb, o_hg, x, w_out[l].astype(BF16), ln1_g[l], ln1_b[l], alpha)
        wc = _router(x, w_router[l], router_bias[l])
        x, xb = _moe(xb, x, wc, w_exp_gate[l].astype(BF16), w_exp_up[l].astype(BF16),
                     w_exp_down[l].astype(BF16), w_sh_gate[l].astype(BF16), w_sh_up[l].astype(BF16),
                     w_sh_down[l].astype(BF16), ln2_g[l], ln2_b[l], alpha)
        kp.append(proj[1, :n_p].reshape(B, S, N_HEADS, HEAD_DIM))
        vp.append(proj[2, :n_p].reshape(B, S, N_HEADS, HEAD_DIM))
        kd.append(proj[1, n_p:].reshape(Bs, Ts, N_HEADS, HEAD_DIM))
        vd.append(proj[2, n_p:].reshape(Bs, Ts, N_HEADS, HEAD_DIM))
        sp.append(s_p)
        sd.append(s_s)
    return (x[:n_p].reshape(B, S, D), x[n_p:].reshape(Bs, Ts, D),
            jnp.stack(kp), jnp.stack(vp), jnp.stack(sp), jnp.stack(kd), jnp.stack(vd), jnp.stack(sd))
```

```python
import functools
import math

import jax
import jax.numpy as jnp
from jax import lax
from jax.experimental import pallas as pl
from jax.experimental.pallas import tpu as pltpu

F32 = jnp.float32
BF16 = jnp.bfloat16

N_HEADS = 8
HEAD_DIM = 128
N_EXPERTS = 64
TOP_K = 8
N_GROUPS = 8
TOPK_GROUPS = 4
GROUP_SIZE = N_EXPERTS // N_GROUPS
ROUTE_SCALE = 2.5
NEG_BIG = -1e9
LB_FLOOR = 1e-30
LN_EPS = 1e-5
RMS_EPS = 1e-6
LOG2E = 1.4426950408889634
N_SPLITS = 9
P_Q, P_HQ, P_HF, P_HI, P_HG, P_GA, P_GB = range(7)

LANES = 128
SUBLANES = 8
VMEM_LIMIT = 48 * 1024 * 1024

HG_CHUNK = 64
HG_SUB = 32
SB_BLOCK = 256
SB_GROUP = 4
WC_LANES = 128


def _pick_tile(n, pref):
    if n <= pref:
        return n
    for t in range(pref, 7, -1):
        if n % t == 0 and t % SUBLANES == 0:
            return t
    raise ValueError(f"no tile for {n}")


def _params(*sem):
    return pltpu.CompilerParams(dimension_semantics=sem, vmem_limit_bytes=VMEM_LIMIT)


def _layer_norm(x, g, b):
    mu = jnp.mean(x, axis=-1, keepdims=True)
    xc = x - mu
    var = jnp.mean(xc * xc, axis=-1, keepdims=True)
    return xc * lax.rsqrt(var + LN_EPS) * g + b


def _sigmoid(x):
    return 1.0 / (1.0 + jnp.exp(-x))


def _log_sigmoid(x):
    return jnp.minimum(x, 0.0) - jnp.log(1.0 + jnp.exp(-jnp.abs(x)))


def _ln_kernel(x_ref, g_ref, b_ref, y_ref, yb_ref):
    y = _layer_norm(x_ref[...], g_ref[...], b_ref[...])
    y_ref[...] = y
    yb_ref[...] = y.astype(BF16)


def _entry_ln(x, g, b):
    T, D = x.shape
    tm = _pick_tile(T, 512)
    return pl.pallas_call(
        _ln_kernel,
        grid=(T // tm,),
        in_specs=[pl.BlockSpec((tm, D), lambda i: (i, 0)),
                  pl.BlockSpec((1, D), lambda i: (0, 0)),
                  pl.BlockSpec((1, D), lambda i: (0, 0))],
        out_specs=[pl.BlockSpec((tm, D), lambda i: (i, 0)),
                   pl.BlockSpec((tm, D), lambda i: (i, 0))],
        out_shape=[jax.ShapeDtypeStruct((T, D), F32), jax.ShapeDtypeStruct((T, D), BF16)],
        compiler_params=_params("parallel"),
        name="entry_ln",
    )(x, g.reshape(1, D), b.reshape(1, D))


def _in_proj_kernel(x_ref, w_ref, o_ref):
    o_ref[...] = jnp.dot(x_ref[...], w_ref[...], preferred_element_type=F32)


def _in_proj(xb, w_in_b):
    T, D = xb.shape
    tm = _pick_tile(T, 1024)
    return pl.pallas_call(
        _in_proj_kernel,
        grid=(N_SPLITS - 2, T // tm),
        in_specs=[pl.BlockSpec((tm, D), lambda j, i: (i, 0)),
                  pl.BlockSpec((D, D), lambda j, i: (0, jnp.where(j >= 1, j + 2, j)))],
        out_specs=pl.BlockSpec((None, tm, D), lambda j, i: (j, i, 0)),
        out_shape=jax.ShapeDtypeStruct((N_SPLITS - 2, T, D), F32),
        compiler_params=_params("parallel", "parallel"),
        name="in_proj",
    )(xb, w_in_b)


def _kv_proj_kernel(x_ref, wk_ref, wv_ref, *refs):
    kb_ref, vb_ref, ko_ref, vo_ref = refs[-4:]
    tm = x_ref.shape[0]
    x = x_ref[...]
    for w_ref, b_ref, o_ref in ((wk_ref, kb_ref, ko_ref), (wv_ref, vb_ref, vo_ref)):
        y = jnp.dot(x, w_ref[...], preferred_element_type=F32)
        b_ref[...] = y.astype(BF16)
        for h in range(N_HEADS):
            o_ref[pl.ds(h, tm, stride=N_HEADS), :] = y[:, h * HEAD_DIM:(h + 1) * HEAD_DIM]


def _kv_proj(xb, w_in_b, k_prev, v_prev, layer, depth, row0, n):
    D = xb.shape[1]
    tm = _pick_tile(n, 512)
    assert row0 % tm == 0
    rb = row0 // tm
    out_spec = pl.BlockSpec((None, tm * N_HEADS, HEAD_DIM), lambda i: (layer, i, 0))
    out_sds = jax.ShapeDtypeStruct((depth, n * N_HEADS, HEAD_DIM), F32)
    args = [xb, w_in_b, w_in_b]
    in_specs = [pl.BlockSpec((tm, D), lambda i: (rb + i, 0)),
                pl.BlockSpec((D, D), lambda i: (0, 1)),
                pl.BlockSpec((D, D), lambda i: (0, 2))]
    aliases = {}
    if k_prev is not None:
        args += [k_prev, v_prev]
        in_specs += [pl.BlockSpec(memory_space=pl.ANY)] * 2
        aliases = {3: 2, 4: 3}
    return pl.pallas_call(
        _kv_proj_kernel,
        grid=(n // tm,),
        in_specs=in_specs,
        out_specs=[pl.BlockSpec((tm, D), lambda i: (i, 0)), pl.BlockSpec((tm, D), lambda i: (i, 0)),
                   out_spec, out_spec],
        out_shape=[jax.ShapeDtypeStruct((n, D), BF16), jax.ShapeDtypeStruct((n, D), BF16),
                   out_sds, out_sds],
        input_output_aliases=aliases,
        compiler_params=_params("parallel"),
        name="kv_proj",
    )(*args)


def _neg_tri(tk):
    r = lax.broadcasted_iota(jnp.int32, (tk, tk), 0)
    c = lax.broadcasted_iota(jnp.int32, (tk, tk), 1)
    return jnp.where(r >= c, -1.0, 0.0).astype(BF16)


def _strict_lower(tq, tk):
    return (lax.broadcasted_iota(jnp.int32, (tq, tk), 1)
            < lax.broadcasted_iota(jnp.int32, (tq, tk), 0))


def _sb_z(qs, kb):
    return lax.dot_general(qs, kb, (((1,), (1,)), ((), ())), preferred_element_type=F32)


def _sb_cumsum_lhs(z, tk, mask):
    sp = jnp.maximum(z, 0.0) + jnp.log(1.0 + jnp.exp2(jnp.abs(z) * (-LOG2E)))
    if mask is not None:
        sp = jnp.where(mask, sp, 0.0)
    sp = sp.astype(BF16)
    n = z.shape[1] // tk
    return sp if n == 1 else jnp.concatenate([sp[:, a * tk:(a + 1) * tk] for a in range(n)], axis=0)


def _sb_weights(z, c, carry, tk, mask):
    tq = z.shape[0]
    n = z.shape[1] // tk
    ws = [None] * n
    for a in range(n - 1, -1, -1):
        ca = c[a * tq:(a + 1) * tq, :]
        w = jnp.exp2((z[:, a * tk:(a + 1) * tk] + ca + carry) * LOG2E)
        if mask is not None:
            w = jnp.where(mask, w, 0.0)
        ws[a] = w.astype(BF16)
        carry = carry + ca[:, 0:1]
    return (ws[0] if n == 1 else jnp.concatenate(ws, axis=1)), carry


def _sb_prompt_kernel(q_ref, kb_scr, vb_scr, o_ref, tri_scr, *, blk):
    i = pl.program_id(2)

    @pl.when(i == 0)
    def _():
        tri_scr[...] = _neg_tri(blk)

    qs = (q_ref[...] * (1.0 / math.sqrt(HEAD_DIM))).astype(BF16)

    def sweep(j_lo, n, co, diag_last):
        carry, o = co
        rows = [pl.ds(pl.multiple_of((j_lo + a) * blk, blk), blk) for a in range(n)]
        masks = [_strict_lower(blk, blk) if diag_last and a == n - 1 else None for a in range(n)]
        zs = [_sb_z(qs, kb_scr[r, :]) for r in rows]
        cs = [jnp.dot(_sb_cumsum_lhs(z, blk, m), tri_scr[...], preferred_element_type=F32)
              for z, m in zip(zs, masks)]
        for a in range(n - 1, -1, -1):
            w, carry = _sb_weights(zs[a], cs[a], carry, blk, masks[a])
            o = o + jnp.dot(w, vb_scr[rows[a], :], preferred_element_type=F32)
        return carry, o

    co = (jnp.zeros((blk, 1), F32), jnp.zeros((blk, HEAD_DIM), F32))
    rem = i % SB_GROUP
    co = lax.switch(rem, [functools.partial(lambda r, co: sweep(i - r, r + 1, co, True), r)
                          for r in range(SB_GROUP)], co)
    co = lax.fori_loop(0, i // SB_GROUP,
                       lambda p, co: sweep(i - rem - SB_GROUP * (p + 1), SB_GROUP, co, False), co)
    o_ref[...] = co[1]


def _sb_prompt(proj, kb, vb, B, S):
    D = N_HEADS * HEAD_DIM
    t_all = proj.shape[1]
    blk = _pick_tile(S, SB_BLOCK)
    nq = S // blk
    return pl.pallas_call(
        functools.partial(_sb_prompt_kernel, blk=blk),
        grid=(B, N_HEADS, nq),
        in_specs=[pl.BlockSpec((None, blk, HEAD_DIM), lambda b, h, i: (P_Q, b * nq + i, h)),
                  pl.BlockSpec((S, HEAD_DIM), lambda b, h, i: (b, h)),
                  pl.BlockSpec((S, HEAD_DIM), lambda b, h, i: (b, h))],
        out_specs=pl.BlockSpec((blk, HEAD_DIM), lambda b, h, i: (b * nq + i, h)),
        out_shape=jax.ShapeDtypeStruct((t_all, D), F32),
        scratch_shapes=[pltpu.VMEM((blk, blk), BF16)],
        compiler_params=_params("parallel", "parallel", "arbitrary"),
        name="sb_prompt",
    )(proj, kb, vb)


def _sb_sample_kernel(q_ref, k_ref, v_ref, pk_ref, pv_ref, o_in_ref, o_ref, *, pblk):
    del o_in_ref
    T = q_ref.shape[0]
    P = pk_ref.shape[0] // N_HEADS

    def head_rows(ref, h):
        return ref[pl.ds(h, P, stride=N_HEADS), :].astype(BF16)

    heads = range(N_HEADS)
    cols = [slice(h * HEAD_DIM, (h + 1) * HEAD_DIM) for h in heads]
    mask = _strict_lower(T, T)
    qs = [(q_ref[:, cols[h]] * (1.0 / math.sqrt(HEAD_DIM))).astype(BF16) for h in heads]
    z_new = [_sb_z(qs[h], k_ref[:, cols[h]]) for h in heads]
    z_past = [_sb_z(qs[h], head_rows(pk_ref, h)) for h in heads]
    lhs_new = jnp.concatenate([_sb_cumsum_lhs(z_new[h], T, mask) for h in heads], axis=0)
    lhs_past = jnp.concatenate([_sb_cumsum_lhs(z_past[h], pblk, None) for h in heads], axis=0)
    c_new = jnp.dot(lhs_new, _neg_tri(T), preferred_element_type=F32)
    c_past = jnp.dot(lhs_past, _neg_tri(pblk), preferred_element_type=F32)
    rows_past = (P // pblk) * T
    for h in heads:
        w_new, carry = _sb_weights(z_new[h], c_new[h * T:(h + 1) * T], jnp.zeros((T, 1), F32), T, mask)
        w_past, _ = _sb_weights(z_past[h], c_past[h * rows_past:(h + 1) * rows_past], carry, pblk, None)
        o_ref[:, cols[h]] = (jnp.dot(w_new, v_ref[:, cols[h]], preferred_element_type=F32)
                             + jnp.dot(w_past, head_rows(pv_ref, h), preferred_element_type=F32))


def _sb_sample(proj, kb, vb, cache_k, cache_v, o_sb, layer, row0, B, T):
    D = N_HEADS * HEAD_DIM
    depth, _, P = cache_k.shape[:3]
    pblk = _pick_tile(P, SB_BLOCK)
    rb = row0 // T
    cache_k = cache_k.reshape(depth, B, P * N_HEADS, HEAD_DIM)
    cache_v = cache_v.reshape(depth, B, P * N_HEADS, HEAD_DIM)
    cache_spec = pl.BlockSpec((None, None, P * N_HEADS, HEAD_DIM), lambda b: (layer, b, 0, 0))
    return pl.pallas_call(
        functools.partial(_sb_sample_kernel, pblk=pblk),
        grid=(B,),
        in_specs=[pl.BlockSpec((None, T, D), lambda b: (P_Q, rb + b, 0)),
                  pl.BlockSpec((T, D), lambda b: (b, 0)),
                  pl.BlockSpec((T, D), lambda b: (b, 0)),
                  cache_spec, cache_spec,
                  pl.BlockSpec(memory_space=pl.ANY)],
        out_specs=pl.BlockSpec((T, D), lambda b: (rb + b, 0)),
        out_shape=jax.ShapeDtypeStruct(o_sb.shape, F32),
        input_output_aliases={5: 0},
        compiler_params=_params("parallel"),
        name="sb_sample",
    )(proj, kb, vb, cache_k, cache_v, o_sb)


def _cumsum_rows(x):
    n = x.shape[0]
    row = lax.broadcasted_iota(jnp.int32, x.shape, 0)
    s = 1
    while s < n:
        x = x + jnp.where(row >= s, pltpu.roll(x, s, 0), 0.0)
        s *= 2
    return x


def _hgrn_kernel(*refs, layer, has_s0):
    if has_s0:
        q_ref, f_ref, i_ref, g_ref, lbl_ref, norm_ref, s0_ref, _, o_ref, sf_ref, st_scr = refs
    else:
        q_ref, f_ref, i_ref, g_ref, lbl_ref, norm_ref, o_ref, sf_ref, st_scr = refs
    c = pl.program_id(1)
    C = q_ref.shape[0]
    n_sub = C // HG_SUB

    @pl.when(c == 0)
    def _():
        for h in range(N_HEADS):
            if has_s0:
                st_scr[h] = s0_ref[h].T
            else:
                st_scr[h] = jnp.zeros((HEAD_DIM, HEAD_DIM), F32)

    lg = lbl_ref[...]
    ex = jnp.exp(lg - jnp.max(lg, axis=0, keepdims=True))
    soft = ex / jnp.sum(ex, axis=0, keepdims=True)
    lb = jnp.zeros((1, lg.shape[1]), F32)
    for m in range(1, layer + 1):
        lb = lb + soft[m:m + 1, :]

    a = f_ref[...]
    x1 = _log_sigmoid(a)
    x2 = _log_sigmoid(-a) + jnp.log(jnp.maximum(lb, LB_FLOOR))
    logf = jnp.maximum(x1, x2) + jnp.log(1.0 + jnp.exp(-jnp.abs(x1 - x2)))
    logf = jnp.minimum(logf, 0.0)
    kk_all = 1.0 - jnp.exp(logf)
    b_all = _cumsum_rows(logf)
    qraw = q_ref[...]
    q_all = qraw * _sigmoid(qraw)
    graw = g_ref[...]
    gate_all = graw * _sigmoid(graw)
    v_all = i_ref[...]

    row = lax.broadcasted_iota(jnp.int32, (C, HEAD_DIM), 0)
    for h in range(N_HEADS):
        hs = slice(h * HEAD_DIM, (h + 1) * HEAD_DIM)
        b = b_all[:, hs]
        q = q_all[:, hs]
        kk = kk_all[:, hs]
        v = v_all[:, hs]
        vb = v.astype(BF16)
        st = st_scr[h]
        qt = (q * jnp.exp(b)).astype(BF16)
        o = lax.dot_general(qt, st.astype(BF16), (((1,), (1,)), ((), ())),
                            preferred_element_type=F32)
        parts = []
        for i in range(n_sub):
            lo_r, hi_r = i * HG_SUB, (i + 1) * HG_SUB
            if i == 0:
                ref_row = jnp.zeros((1, HEAD_DIM), F32)
            else:
                ref_row = b[lo_r - 1:lo_r, :]
            qh = (q[lo_r:hi_r] * jnp.exp(b[lo_r:hi_r] - ref_row)).astype(BF16)
            kh = (kk[0:hi_r] * jnp.exp(ref_row - b[0:hi_r])).astype(BF16)
            sc = lax.dot_general(qh, kh, (((1,), (1,)), ((), ())),
                                 preferred_element_type=F32)
            tpos = lax.broadcasted_iota(jnp.int32, (HG_SUB, hi_r), 0) + lo_r
            spos = lax.broadcasted_iota(jnp.int32, (HG_SUB, hi_r), 1)
            sc = jnp.where(spos <= tpos, sc, 0.0)
            parts.append(jnp.dot(sc.astype(BF16), vb[0:hi_r], preferred_element_type=F32))
        o = o + jnp.concatenate(parts, axis=0)
        b_last = b[C - 1:C, :]
        ke = (kk * jnp.exp(b_last - b)).astype(BF16)
        upd = lax.dot_general(vb, ke, (((0,), (0,)), ((), ())), preferred_element_type=F32)
        st_scr[h] = st * jnp.exp(b_last) + upd
        ms = jnp.mean(o * o, axis=-1, keepdims=True)
        o_ref[:, hs] = o * lax.rsqrt(ms + RMS_EPS) * norm_ref[...] * gate_all[:, hs]

    @pl.when(c == pl.num_programs(1) - 1)
    def _():
        for h in range(N_HEADS):
            sf_ref[h] = st_scr[h].T


def _hgrn(proj, lb_logits, hg_norm, s0, o_prev, layer, row0, B, T):
    D = N_HEADS * HEAD_DIM
    C = HG_CHUNK
    assert T % C == 0 and row0 % C == 0
    nc = T // C
    rb = row0 // C
    depth = lb_logits.shape[0]
    t_all = proj.shape[1]

    def act_spec(split):
        return pl.BlockSpec((None, C, D), lambda b, c: (split, rb + b * nc + c, 0))

    in_specs = [act_spec(P_HQ), act_spec(P_HF), act_spec(P_HI), act_spec(P_HG),
                pl.BlockSpec((depth, D), lambda b, c: (0, 0)),
                pl.BlockSpec((1, HEAD_DIM), lambda b, c: (0, 0))]
    args = [proj, proj, proj, proj, lb_logits, hg_norm.reshape(1, HEAD_DIM)]
    if s0 is not None:
        in_specs.append(pl.BlockSpec((None, N_HEADS, HEAD_DIM, HEAD_DIM), lambda b, c: (b, 0, 0, 0)))
        in_specs.append(pl.BlockSpec(memory_space=pl.ANY))
        args += [s0, o_prev]
    return pl.pallas_call(
        functools.partial(_hgrn_kernel, layer=layer, has_s0=s0 is not None),
        grid=(B, nc),
        in_specs=in_specs,
        out_specs=[pl.BlockSpec((C, D), lambda b, c: (rb + b * nc + c, 0)),
                   pl.BlockSpec((None, N_HEADS, HEAD_DIM, HEAD_DIM), lambda b, c: (b, 0, 0, 0))],
        out_shape=[jax.ShapeDtypeStruct((t_all, D), F32),
                   jax.ShapeDtypeStruct((B, N_HEADS, HEAD_DIM, HEAD_DIM), F32)],
        input_output_aliases={} if s0 is None else {len(args) - 1: 0},
        scratch_shapes=[pltpu.VMEM((N_HEADS, HEAD_DIM, HEAD_DIM), F32)],
        compiler_params=_params("parallel", "arbitrary"),
        name="hgrn2",
    )(*args)


def _merge_out_kernel(ga_ref, gb_ref, oa_ref, ob_ref, x_ref, w_ref, g_ref, b_ref, y_ref, yb_ref, *, alpha):
    merged = _sigmoid(ga_ref[...]) * oa_ref[...] + _sigmoid(gb_ref[...]) * ob_ref[...]
    r = alpha * x_ref[...] + jnp.dot(merged.astype(BF16), w_ref[...], preferred_element_type=F32)
    y = _layer_norm(r, g_ref[...], b_ref[...])
    y_ref[...] = y
    yb_ref[...] = y.astype(BF16)


def _merge_out(proj, o_sb, o_hg, x, w_out_b, g, b, alpha):
    T, D = x.shape
    tm = _pick_tile(T, 256)
    row = pl.BlockSpec((tm, D), lambda i: (i, 0))
    vec = pl.BlockSpec((1, D), lambda i: (0, 0))
    return pl.pallas_call(
        functools.partial(_merge_out_kernel, alpha=alpha),
        grid=(T // tm,),
        in_specs=[pl.BlockSpec((None, tm, D), lambda i: (P_GA, i, 0)),
                  pl.BlockSpec((None, tm, D), lambda i: (P_GB, i, 0)),
                  row, row, row,
                  pl.BlockSpec((D, D), lambda i: (0, 0)), vec, vec],
        out_specs=[row, row],
        out_shape=[jax.ShapeDtypeStruct((T, D), F32), jax.ShapeDtypeStruct((T, D), BF16)],
        compiler_params=_params("parallel"),
        name="merge_out",
    )(proj, proj, o_sb, o_hg, x, w_out_b, g.reshape(1, D), b.reshape(1, D))


def _router_kernel(x_ref, wr_ref, bias_ref, wc_ref):
    tm = x_ref.shape[0]
    logits = lax.dot_general(wr_ref[...], x_ref[...], (((1,), (1,)), ((), ())),
                             precision=lax.Precision.HIGHEST, preferred_element_type=F32)
    s = _sigmoid(logits)
    sb = s + bias_ref[...]
    sub = lax.broadcasted_iota(jnp.int32, (GROUP_SIZE, tm), 0)
    gs = []
    for g in range(N_GROUPS):
        blk = sb[g * GROUP_SIZE:(g + 1) * GROUP_SIZE, :]
        m1 = jnp.max(blk, axis=0, keepdims=True)
        first = jnp.min(jnp.where(blk == m1, sub, GROUP_SIZE), axis=0, keepdims=True)
        m2 = jnp.max(jnp.where(sub == first, -jnp.inf, blk), axis=0, keepdims=True)
        gs.append(m1 + m2)
    masked = []
    for g in range(N_GROUPS):
        beat = jnp.zeros((1, tm), jnp.int32)
        for g2 in range(N_GROUPS):
            if g2 == g:
                continue
            if g2 < g:
                beat = beat + (gs[g2] >= gs[g]).astype(jnp.int32)
            else:
                beat = beat + (gs[g2] > gs[g]).astype(jnp.int32)
        keep = beat < TOPK_GROUPS
        masked.append(jnp.where(keep, sb[g * GROUP_SIZE:(g + 1) * GROUP_SIZE, :], NEG_BIG))
    m = jnp.concatenate(masked, axis=0)
    erow = lax.broadcasted_iota(jnp.int32, (N_EXPERTS, tm), 0)
    sel = jnp.zeros((N_EXPERTS, tm), jnp.bool_)
    for _ in range(TOP_K):
        mx = jnp.max(m, axis=0, keepdims=True)
        first = jnp.min(jnp.where(m == mx, erow, N_EXPERTS), axis=0, keepdims=True)
        hit = erow == first
        sel = jnp.logical_or(sel, hit)
        m = jnp.where(hit, -jnp.inf, m)
    w = jnp.where(sel, s, 0.0)
    w = w / jnp.sum(w, axis=0, keepdims=True) * ROUTE_SCALE
    pad_rows = lax.broadcasted_iota(jnp.int32, (WC_LANES - N_EXPERTS, tm), 0)
    tail = jnp.where(pad_rows == 0, 1.0, 0.0).astype(F32)
    wc_ref[...] = jnp.concatenate([w, tail], axis=0).T


def _router(x, w_router, bias):
    T, D = x.shape
    tm = _pick_tile(T, 256)
    return pl.pallas_call(
        _router_kernel,
        grid=(T // tm,),
        in_specs=[pl.BlockSpec((tm, D), lambda i: (i, 0)),
                  pl.BlockSpec((N_EXPERTS, D), lambda i: (0, 0)),
                  pl.BlockSpec((N_EXPERTS, 1), lambda i: (0, 0))],
        out_specs=pl.BlockSpec((tm, WC_LANES), lambda i: (i, 0)),
        out_shape=jax.ShapeDtypeStruct((T, WC_LANES), F32),
        compiler_params=_params("parallel"),
        name="router",
    )(x, w_router.T, bias.reshape(N_EXPERTS, 1))


def _moe_kernel(xb_ref, x_ref, wc_ref, wg_ref, wu_ref, wd_ref, sg_ref, su_ref, sd_ref, g_ref, b_ref,
                y_ref, yb_ref, acc_ref, *, alpha, n_routed):
    e = pl.program_id(1)

    @pl.when(e == 0)
    def _():
        acc_ref[...] = jnp.zeros_like(acc_ref)

    def expert(wg, wu, wd):
        xb = xb_ref[...]
        lane = lax.broadcasted_iota(jnp.int32, wc_ref.shape, 1)
        col = jnp.sum(jnp.where(lane == e, wc_ref[...], 0.0), axis=1, keepdims=True)
        hg = jnp.dot(xb, wg[...], preferred_element_type=F32)
        hu = jnp.dot(xb, wu[...], preferred_element_type=F32)
        h = hg * _sigmoid(hg) * hu * col
        acc_ref[...] += jnp.dot(h.astype(BF16), wd[...], preferred_element_type=F32)

    @pl.when(e < n_routed)
    def _():
        expert(wg_ref, wu_ref, wd_ref)

    @pl.when(e == n_routed)
    def _():
        expert(sg_ref, su_ref, sd_ref)

    @pl.when(e == pl.num_programs(1) - 1)
    def _():
        y = _layer_norm(alpha * x_ref[...] + acc_ref[...], g_ref[...], b_ref[...])
        y_ref[...] = y
        yb_ref[...] = y.astype(BF16)


def _moe(xb, x, wc, wg, wu, wd, sg, su, sd, g, b, alpha):
    T, D = x.shape
    n_e, _, F = wg.shape
    tm = _pick_tile(T, 1024)
    row = pl.BlockSpec((tm, D), lambda i, e: (i, 0))
    vec = pl.BlockSpec((1, D), lambda i, e: (0, 0))
    last = n_e - 1
    return pl.pallas_call(
        functools.partial(_moe_kernel, alpha=alpha, n_routed=n_e),
        grid=(T // tm, n_e + 1),
        in_specs=[row, row,
                  pl.BlockSpec((tm, WC_LANES), lambda i, e: (i, 0)),
                  pl.BlockSpec((None, D, F), lambda i, e: (jnp.minimum(e, last), 0, 0)),
                  pl.BlockSpec((None, D, F), lambda i, e: (jnp.minimum(e, last), 0, 0)),
                  pl.BlockSpec((None, F, D), lambda i, e: (jnp.minimum(e, last), 0, 0)),
                  pl.BlockSpec((D, F), lambda i, e: (0, 0)),
                  pl.BlockSpec((D, F), lambda i, e: (0, 0)),
                  pl.BlockSpec((F, D), lambda i, e: (0, 0)),
                  vec, vec],
        out_specs=[row, row],
        out_shape=[jax.ShapeDtypeStruct((T, D), F32), jax.ShapeDtypeStruct((T, D), BF16)],
        scratch_shapes=[pltpu.VMEM((tm, D), F32)],
        compiler_params=_params("parallel", "arbitrary"),
        name="moe",
    )(xb, x, wc, wg, wu, wd, sg, su, sd, g.reshape(1, D), b.reshape(1, D))


def kernel(x_prompt, x_sample, cache_sb_k, cache_sb_v, state_hgrn, ln_in_g, ln_in_b, w_in, w_out, hg_norm, hg_lb_logits, ln1_g, ln1_b, w_router, router_bias, w_exp_gate, w_exp_up, w_exp_down, w_sh_gate, w_sh_up, w_sh_down, ln2_g, ln2_b):
    B, S, D = x_prompt.shape
    Bs, Ts, _ = x_sample.shape
    depth = w_in.shape[0]
    P = cache_sb_k.shape[2]
    n_p = B * S
    alpha = (2 * depth) ** 0.25

    x_all = jnp.concatenate([x_prompt.reshape(n_p, D), x_sample.reshape(Bs * Ts, D)], axis=0)
    x, xb = _entry_ln(x_all, ln_in_g, ln_in_b)

    sp, sd = [], []
    kp = vp = kd = vd = None
    for l in range(depth):
        w_in_b = w_in[l].astype(BF16)
        proj = _in_proj(xb, w_in_b)
        kb_p, vb_p, kp, vp = _kv_proj(xb, w_in_b, kp, vp, l, depth, 0, n_p)
        kb_s, vb_s, kd, vd = _kv_proj(xb, w_in_b, kd, vd, l, depth, n_p, Bs * Ts)
        o_sb = _sb_prompt(proj, kb_p, vb_p, B, S)
        o_sb = _sb_sample(proj, kb_s, vb_s, cache_sb_k, cache_sb_v, o_sb, l, n_p, Bs, Ts)
        o_hg, s_p = _hgrn(proj, hg_lb_logits, hg_norm[l], None, None, l, 0, B, S)
        o_hg, s_s = _hgrn(proj, hg_lb_logits, hg_norm[l], state_hgrn[l], o_hg, l, n_p, Bs, Ts)
        x, xb = _merge_out(proj, o_sb, o_hg, x, w_out[l].astype(BF16), ln1_g[l], ln1_b[l], alpha)
        wc = _router(x, w_router[l], router_bias[l])
        x, xb = _moe(xb, x, wc, w_exp_gate[l].astype(BF16), w_exp_up[l].astype(BF16),
                     w_exp_down[l].astype(BF16), w_sh_gate[l].astype(BF16), w_sh_up[l].astype(BF16),
                     w_sh_down[l].astype(BF16), ln2_g[l], ln2_b[l], alpha)
        sp.append(s_p)
        sd.append(s_s)
    kv_p = (depth, B, S, N_HEADS, HEAD_DIM)
    kv_s = (depth, Bs, Ts, N_HEADS, HEAD_DIM)
    return (x[:n_p].reshape(B, S, D), x[n_p:].reshape(Bs, Ts, D),
            kp.reshape(kv_p), vp.reshape(kv_p), jnp.stack(sp),
            kd.reshape(kv_s), vd.reshape(kv_s), jnp.stack(sd))
```

```python
import functools
import math

import jax
import jax.numpy as jnp
from jax import lax
from jax.experimental import pallas as pl
from jax.experimental.pallas import tpu as pltpu

F32 = jnp.float32
BF16 = jnp.bfloat16

N_HEADS = 8
HEAD_DIM = 128
N_EXPERTS = 64
TOP_K = 8
N_GROUPS = 8
TOPK_GROUPS = 4
GROUP_SIZE = N_EXPERTS // N_GROUPS
ROUTE_SCALE = 2.5
NEG_BIG = -1e9
LB_FLOOR = 1e-30
LN_EPS = 1e-5
RMS_EPS = 1e-6
LOG2E = 1.4426950408889634
N_SPLITS = 9
P_Q, P_HQ, P_HF, P_HI, P_HG, P_GA, P_GB = range(7)

LANES = 128
SUBLANES = 8
VMEM_LIMIT = 48 * 1024 * 1024

HG_CHUNK = 64
HG_SUB = 32
SB_BLOCK = 256
SB_GROUP = 4
WC_LANES = 128
MOE_TILE = 256
MOE_CAP = 64
MOE_EGROUP = 8
MOE_TILES_PER_STEP = 24


def _pick_tile(n, pref):
    if n <= pref:
        return n
    for t in range(pref, 7, -1):
        if n % t == 0 and t % SUBLANES == 0:
            return t
    raise ValueError(f"no tile for {n}")


def _params(*sem):
    return pltpu.CompilerParams(dimension_semantics=sem, vmem_limit_bytes=VMEM_LIMIT)


def _layer_norm(x, g, b):
    mu = jnp.mean(x, axis=-1, keepdims=True)
    xc = x - mu
    var = jnp.mean(xc * xc, axis=-1, keepdims=True)
    return xc * lax.rsqrt(var + LN_EPS) * g + b


def _sigmoid(x):
    return 1.0 / (1.0 + jnp.exp(-x))


def _log_sigmoid(x):
    return jnp.minimum(x, 0.0) - jnp.log(1.0 + jnp.exp(-jnp.abs(x)))


def _ln_kernel(x_ref, g_ref, b_ref, y_ref, yb_ref):
    y = _layer_norm(x_ref[...], g_ref[...], b_ref[...])
    y_ref[...] = y
    yb_ref[...] = y.astype(BF16)


def _entry_ln(x, g, b):
    T, D = x.shape
    tm = _pick_tile(T, 512)
    return pl.pallas_call(
        _ln_kernel,
        grid=(T // tm,),
        in_specs=[pl.BlockSpec((tm, D), lambda i: (i, 0)),
                  pl.BlockSpec((1, D), lambda i: (0, 0)),
                  pl.BlockSpec((1, D), lambda i: (0, 0))],
        out_specs=[pl.BlockSpec((tm, D), lambda i: (i, 0)),
                   pl.BlockSpec((tm, D), lambda i: (i, 0))],
        out_shape=[jax.ShapeDtypeStruct((T, D), F32), jax.ShapeDtypeStruct((T, D), BF16)],
        compiler_params=_params("parallel"),
        name="entry_ln",
    )(x, g.reshape(1, D), b.reshape(1, D))


def _in_proj_kernel(x_ref, w_ref, o_ref):
    o_ref[...] = jnp.dot(x_ref[...], w_ref[...], preferred_element_type=F32)


def _in_proj(xb, w_in_b):
    T, D = xb.shape
    tm = _pick_tile(T, 1024)
    return pl.pallas_call(
        _in_proj_kernel,
        grid=(N_SPLITS - 2, T // tm),
        in_specs=[pl.BlockSpec((tm, D), lambda j, i: (i, 0)),
                  pl.BlockSpec((D, D), lambda j, i: (0, jnp.where(j >= 1, j + 2, j)))],
        out_specs=pl.BlockSpec((None, tm, D), lambda j, i: (j, i, 0)),
        out_shape=jax.ShapeDtypeStruct((N_SPLITS - 2, T, D), F32),
        compiler_params=_params("parallel", "parallel"),
        name="in_proj",
    )(xb, w_in_b)


def _kv_proj_kernel(x_ref, wk_ref, wv_ref, *refs):
    kb_ref, vb_ref, ko_ref, vo_ref = refs[-4:]
    tm = x_ref.shape[0]
    x = x_ref[...]
    for w_ref, b_ref, o_ref in ((wk_ref, kb_ref, ko_ref), (wv_ref, vb_ref, vo_ref)):
        y = jnp.dot(x, w_ref[...], preferred_element_type=F32)
        b_ref[...] = y.astype(BF16)
        for h in range(N_HEADS):
            o_ref[pl.ds(h, tm, stride=N_HEADS), :] = y[:, h * HEAD_DIM:(h + 1) * HEAD_DIM]


def _kv_proj(xb, w_in_b, k_prev, v_prev, layer, depth, row0, n):
    D = xb.shape[1]
    tm = _pick_tile(n, 512)
    assert row0 % tm == 0
    rb = row0 // tm
    out_spec = pl.BlockSpec((None, tm * N_HEADS, HEAD_DIM), lambda i: (layer, i, 0))
    out_sds = jax.ShapeDtypeStruct((depth, n * N_HEADS, HEAD_DIM), F32)
    args = [xb, w_in_b, w_in_b]
    in_specs = [pl.BlockSpec((tm, D), lambda i: (rb + i, 0)),
                pl.BlockSpec((D, D), lambda i: (0, 1)),
                pl.BlockSpec((D, D), lambda i: (0, 2))]
    aliases = {}
    if k_prev is not None:
        args += [k_prev, v_prev]
        in_specs += [pl.BlockSpec(memory_space=pl.ANY)] * 2
        aliases = {3: 2, 4: 3}
    return pl.pallas_call(
        _kv_proj_kernel,
        grid=(n // tm,),
        in_specs=in_specs,
        out_specs=[pl.BlockSpec((tm, D), lambda i: (i, 0)), pl.BlockSpec((tm, D), lambda i: (i, 0)),
                   out_spec, out_spec],
        out_shape=[jax.ShapeDtypeStruct((n, D), BF16), jax.ShapeDtypeStruct((n, D), BF16),
                   out_sds, out_sds],
        input_output_aliases=aliases,
        compiler_params=_params("parallel"),
        name="kv_proj",
    )(*args)


def _neg_tri(tk):
    r = lax.broadcasted_iota(jnp.int32, (tk, tk), 0)
    c = lax.broadcasted_iota(jnp.int32, (tk, tk), 1)
    return jnp.where(r >= c, -1.0, 0.0).astype(BF16)


def _strict_lower(tq, tk):
    return (lax.broadcasted_iota(jnp.int32, (tq, tk), 1)
            < lax.broadcasted_iota(jnp.int32, (tq, tk), 0))


def _sb_z(qs, kb):
    return lax.dot_general(qs, kb, (((1,), (1,)), ((), ())), preferred_element_type=F32)


def _sb_cumsum_lhs(z, tk, mask):
    sp = jnp.maximum(z, 0.0) + jnp.log(1.0 + jnp.exp2(jnp.abs(z) * (-LOG2E)))
    if mask is not None:
        sp = jnp.where(mask, sp, 0.0)
    sp = sp.astype(BF16)
    n = z.shape[1] // tk
    return sp if n == 1 else jnp.concatenate([sp[:, a * tk:(a + 1) * tk] for a in range(n)], axis=0)


def _sb_weights(z, c, carry, tk, mask):
    tq = z.shape[0]
    n = z.shape[1] // tk
    ws = [None] * n
    for a in range(n - 1, -1, -1):
        ca = c[a * tq:(a + 1) * tq, :]
        w = jnp.exp2((z[:, a * tk:(a + 1) * tk] + ca + carry) * LOG2E)
        if mask is not None:
            w = jnp.where(mask, w, 0.0)
        ws[a] = w.astype(BF16)
        carry = carry + ca[:, 0:1]
    return (ws[0] if n == 1 else jnp.concatenate(ws, axis=1)), carry


def _sb_prompt_kernel(q_ref, kb_scr, vb_scr, o_ref, tri_scr, *, blk):
    i = pl.program_id(2)

    @pl.when(i == 0)
    def _():
        tri_scr[...] = _neg_tri(blk)

    qs = (q_ref[...] * (1.0 / math.sqrt(HEAD_DIM))).astype(BF16)

    def block_rows(j):
        return pl.ds(pl.multiple_of(j * blk, blk), blk)

    def sweep(j_lo, n, co, diag_last, zs=None):
        carry, o = co
        rows = [block_rows(j_lo + a) for a in range(n)]
        masks = [_strict_lower(blk, blk) if diag_last and a == n - 1 else None for a in range(n)]
        if zs is None:
            zs = [_sb_z(qs, kb_scr[r, :]) for r in rows]
        cs = [jnp.dot(_sb_cumsum_lhs(z, blk, m), tri_scr[...], preferred_element_type=F32)
              for z, m in zip(zs, masks)]
        for a in range(n - 1, -1, -1):
            w, carry = _sb_weights(zs[a], cs[a], carry, blk, masks[a])
            o = o + jnp.dot(w, vb_scr[rows[a], :], preferred_element_type=F32)
        return carry, o

    co = (jnp.zeros((blk, 1), F32), jnp.zeros((blk, HEAD_DIM), F32))
    rem = i % SB_GROUP
    co = lax.switch(rem, [functools.partial(lambda r, co: sweep(i - r, r + 1, co, True), r)
                          for r in range(SB_GROUP)], co)
    co = lax.fori_loop(0, i // SB_GROUP,
                       lambda p, co: sweep(i - rem - SB_GROUP * (p + 1), SB_GROUP, co, False), co)
    o_ref[...] = co[1]


def _sb_prompt(proj, kb, vb, B, S):
    D = N_HEADS * HEAD_DIM
    t_all = proj.shape[1]
    blk = _pick_tile(S, SB_BLOCK)
    nq = S // blk
    return pl.pallas_call(
        functools.partial(_sb_prompt_kernel, blk=blk),
        grid=(B, N_HEADS, nq),
        in_specs=[pl.BlockSpec((None, blk, HEAD_DIM), lambda b, h, i: (P_Q, b * nq + i, h)),
                  pl.BlockSpec((S, HEAD_DIM), lambda b, h, i: (b, h)),
                  pl.BlockSpec((S, HEAD_DIM), lambda b, h, i: (b, h))],
        out_specs=pl.BlockSpec((blk, HEAD_DIM), lambda b, h, i: (b * nq + i, h)),
        out_shape=jax.ShapeDtypeStruct((t_all, D), F32),
        scratch_shapes=[pltpu.VMEM((blk, blk), BF16)],
        compiler_params=_params("parallel", "parallel", "arbitrary"),
        name="sb_prompt",
    )(proj, kb, vb)


def _sb_sample_kernel(q_ref, k_ref, v_ref, pk_ref, pv_ref, o_in_ref, o_ref, *, pblk):
    del o_in_ref
    T = q_ref.shape[0]
    P = pk_ref.shape[0] // N_HEADS

    def head_rows(ref, h):
        return ref[pl.ds(h, P, stride=N_HEADS), :].astype(BF16)

    heads = range(N_HEADS)
    cols = [slice(h * HEAD_DIM, (h + 1) * HEAD_DIM) for h in heads]
    mask = _strict_lower(T, T)
    qs = [(q_ref[:, cols[h]] * (1.0 / math.sqrt(HEAD_DIM))).astype(BF16) for h in heads]
    z_new = [_sb_z(qs[h], k_ref[:, cols[h]]) for h in heads]
    z_past = [_sb_z(qs[h], head_rows(pk_ref, h)) for h in heads]
    lhs_new = jnp.concatenate([_sb_cumsum_lhs(z_new[h], T, mask) for h in heads], axis=0)
    lhs_past = jnp.concatenate([_sb_cumsum_lhs(z_past[h], pblk, None) for h in heads], axis=0)
    c_new = jnp.dot(lhs_new, _neg_tri(T), preferred_element_type=F32)
    c_past = jnp.dot(lhs_past, _neg_tri(pblk), preferred_element_type=F32)
    rows_past = (P // pblk) * T
    for h in heads:
        w_new, carry = _sb_weights(z_new[h], c_new[h * T:(h + 1) * T], jnp.zeros((T, 1), F32), T, mask)
        w_past, _ = _sb_weights(z_past[h], c_past[h * rows_past:(h + 1) * rows_past], carry, pblk, None)
        o_ref[:, cols[h]] = (jnp.dot(w_new, v_ref[:, cols[h]], preferred_element_type=F32)
                             + jnp.dot(w_past, head_rows(pv_ref, h), preferred_element_type=F32))


def _sb_sample(proj, kb, vb, cache_k, cache_v, o_sb, layer, row0, B, T):
    D = N_HEADS * HEAD_DIM
    depth, _, P = cache_k.shape[:3]
    pblk = _pick_tile(P, SB_BLOCK)
    rb = row0 // T
    cache_k = cache_k.reshape(depth, B, P * N_HEADS, HEAD_DIM)
    cache_v = cache_v.reshape(depth, B, P * N_HEADS, HEAD_DIM)
    cache_spec = pl.BlockSpec((None, None, P * N_HEADS, HEAD_DIM), lambda b: (layer, b, 0, 0))
    return pl.pallas_call(
        functools.partial(_sb_sample_kernel, pblk=pblk),
        grid=(B,),
        in_specs=[pl.BlockSpec((None, T, D), lambda b: (P_Q, rb + b, 0)),
                  pl.BlockSpec((T, D), lambda b: (b, 0)),
                  pl.BlockSpec((T, D), lambda b: (b, 0)),
                  cache_spec, cache_spec,
                  pl.BlockSpec(memory_space=pl.ANY)],
        out_specs=pl.BlockSpec((T, D), lambda b: (rb + b, 0)),
        out_shape=jax.ShapeDtypeStruct(o_sb.shape, F32),
        input_output_aliases={5: 0},
        compiler_params=_params("parallel"),
        name="sb_sample",
    )(proj, kb, vb, cache_k, cache_v, o_sb)


def _cumsum_rows(x):
    n = x.shape[0]
    row = lax.broadcasted_iota(jnp.int32, x.shape, 0)
    s = 1
    while s < n:
        x = x + jnp.where(row >= s, pltpu.roll(x, s, 0), 0.0)
        s *= 2
    return x


def _hgrn_kernel(*refs, layer, has_s0):
    if has_s0:
        q_ref, f_ref, i_ref, g_ref, lbl_ref, norm_ref, s0_ref, _, o_ref, sf_ref, st_scr = refs
    else:
        q_ref, f_ref, i_ref, g_ref, lbl_ref, norm_ref, o_ref, sf_ref, st_scr = refs
    c = pl.program_id(1)
    C = q_ref.shape[0]
    n_sub = C // HG_SUB

    @pl.when(c == 0)
    def _():
        for h in range(N_HEADS):
            if has_s0:
                st_scr[h] = s0_ref[h].T
            else:
                st_scr[h] = jnp.zeros((HEAD_DIM, HEAD_DIM), F32)

    lg = lbl_ref[...]
    ex = jnp.exp(lg - jnp.max(lg, axis=0, keepdims=True))
    soft = ex / jnp.sum(ex, axis=0, keepdims=True)
    lb = jnp.zeros((1, lg.shape[1]), F32)
    for m in range(1, layer + 1):
        lb = lb + soft[m:m + 1, :]

    a = f_ref[...]
    x1 = _log_sigmoid(a)
    x2 = _log_sigmoid(-a) + jnp.log(jnp.maximum(lb, LB_FLOOR))
    logf = jnp.maximum(x1, x2) + jnp.log(1.0 + jnp.exp(-jnp.abs(x1 - x2)))
    logf = jnp.minimum(logf, 0.0)
    kk_all = 1.0 - jnp.exp(logf)
    b_all = _cumsum_rows(logf)
    qraw = q_ref[...]
    q_all = qraw * _sigmoid(qraw)
    graw = g_ref[...]
    gate_all = graw * _sigmoid(graw)
    v_all = i_ref[...]

    row = lax.broadcasted_iota(jnp.int32, (C, HEAD_DIM), 0)
    for h in range(N_HEADS):
        hs = slice(h * HEAD_DIM, (h + 1) * HEAD_DIM)
        b = b_all[:, hs]
        q = q_all[:, hs]
        kk = kk_all[:, hs]
        v = v_all[:, hs]
        vb = v.astype(BF16)
        st = st_scr[h]
        qt = (q * jnp.exp(b)).astype(BF16)
        o = lax.dot_general(qt, st.astype(BF16), (((1,), (1,)), ((), ())),
                            preferred_element_type=F32)
        parts = []
        for i in range(n_sub):
            lo_r, hi_r = i * HG_SUB, (i + 1) * HG_SUB
            if i == 0:
                ref_row = jnp.zeros((1, HEAD_DIM), F32)
            else:
                ref_row = b[lo_r - 1:lo_r, :]
            qh = (q[lo_r:hi_r] * jnp.exp(b[lo_r:hi_r] - ref_row)).astype(BF16)
            kh = (kk[0:hi_r] * jnp.exp(ref_row - b[0:hi_r])).astype(BF16)
            sc = lax.dot_general(qh, kh, (((1,), (1,)), ((), ())),
                                 preferred_element_type=F32)
            tpos = lax.broadcasted_iota(jnp.int32, (HG_SUB, hi_r), 0) + lo_r
            spos = lax.broadcasted_iota(jnp.int32, (HG_SUB, hi_r), 1)
            sc = jnp.where(spos <= tpos, sc, 0.0)
            parts.append(jnp.dot(sc.astype(BF16), vb[0:hi_r], preferred_element_type=F32))
        o = o + jnp.concatenate(parts, axis=0)
        b_last = b[C - 1:C, :]
        ke = (kk * jnp.exp(b_last - b)).astype(BF16)
        upd = lax.dot_general(vb, ke, (((0,), (0,)), ((), ())), preferred_element_type=F32)
        st_scr[h] = st * jnp.exp(b_last) + upd
        ms = jnp.mean(o * o, axis=-1, keepdims=True)
        o_ref[:, hs] = o * lax.rsqrt(ms + RMS_EPS) * norm_ref[...] * gate_all[:, hs]

    @pl.when(c == pl.num_programs(1) - 1)
    def _():
        for h in range(N_HEADS):
            sf_ref[h] = st_scr[h].T


def _hgrn(proj, lb_logits, hg_norm, s0, o_prev, layer, row0, B, T):
    D = N_HEADS * HEAD_DIM
    C = HG_CHUNK
    assert T % C == 0 and row0 % C == 0
    nc = T // C
    rb = row0 // C
    depth = lb_logits.shape[0]
    t_all = proj.shape[1]

    def act_spec(split):
        return pl.BlockSpec((None, C, D), lambda b, c: (split, rb + b * nc + c, 0))

    in_specs = [act_spec(P_HQ), act_spec(P_HF), act_spec(P_HI), act_spec(P_HG),
                pl.BlockSpec((depth, D), lambda b, c: (0, 0)),
                pl.BlockSpec((1, HEAD_DIM), lambda b, c: (0, 0))]
    args = [proj, proj, proj, proj, lb_logits, hg_norm.reshape(1, HEAD_DIM)]
    if s0 is not None:
        in_specs.append(pl.BlockSpec((None, N_HEADS, HEAD_DIM, HEAD_DIM), lambda b, c: (b, 0, 0, 0)))
        in_specs.append(pl.BlockSpec(memory_space=pl.ANY))
        args += [s0, o_prev]
    return pl.pallas_call(
        functools.partial(_hgrn_kernel, layer=layer, has_s0=s0 is not None),
        grid=(B, nc),
        in_specs=in_specs,
        out_specs=[pl.BlockSpec((C, D), lambda b, c: (rb + b * nc + c, 0)),
                   pl.BlockSpec((None, N_HEADS, HEAD_DIM, HEAD_DIM), lambda b, c: (b, 0, 0, 0))],
        out_shape=[jax.ShapeDtypeStruct((t_all, D), F32),
                   jax.ShapeDtypeStruct((B, N_HEADS, HEAD_DIM, HEAD_DIM), F32)],
        input_output_aliases={} if s0 is None else {len(args) - 1: 0},
        scratch_shapes=[pltpu.VMEM((N_HEADS, HEAD_DIM, HEAD_DIM), F32)],
        compiler_params=_params("parallel", "arbitrary"),
        name="hgrn2",
    )(*args)


def _merge_out_kernel(ga_ref, gb_ref, oa_ref, ob_ref, x_ref, w_ref, g_ref, b_ref, y_ref, yb_ref, *, alpha):
    merged = _sigmoid(ga_ref[...]) * oa_ref[...] + _sigmoid(gb_ref[...]) * ob_ref[...]
    r = alpha * x_ref[...] + jnp.dot(merged.astype(BF16), w_ref[...], preferred_element_type=F32)
    y = _layer_norm(r, g_ref[...], b_ref[...])
    y_ref[...] = y
    yb_ref[...] = y.astype(BF16)


def _merge_out(proj, o_sb, o_hg, x, w_out_b, g, b, alpha):
    T, D = x.shape
    tm = _pick_tile(T, 256)
    row = pl.BlockSpec((tm, D), lambda i: (i, 0))
    vec = pl.BlockSpec((1, D), lambda i: (0, 0))
    return pl.pallas_call(
        functools.partial(_merge_out_kernel, alpha=alpha),
        grid=(T // tm,),
        in_specs=[pl.BlockSpec((None, tm, D), lambda i: (P_GA, i, 0)),
                  pl.BlockSpec((None, tm, D), lambda i: (P_GB, i, 0)),
                  row, row, row,
                  pl.BlockSpec((D, D), lambda i: (0, 0)), vec, vec],
        out_specs=[row, row],
        out_shape=[jax.ShapeDtypeStruct((T, D), F32), jax.ShapeDtypeStruct((T, D), BF16)],
        compiler_params=_params("parallel"),
        name="merge_out",
    )(proj, proj, o_sb, o_hg, x, w_out_b, g.reshape(1, D), b.reshape(1, D))


def _router_kernel(x_ref, wr_ref, bias_ref, wc_ref, wo_ref, pos_ref, cnt_ref):
    tm = x_ref.shape[0]
    logits = lax.dot_general(wr_ref[...], x_ref[...], (((1,), (1,)), ((), ())),
                             precision=lax.Precision.HIGHEST, preferred_element_type=F32)
    s = _sigmoid(logits)
    sb = s + bias_ref[...]
    sub = lax.broadcasted_iota(jnp.int32, (GROUP_SIZE, tm), 0)
    gs = []
    for g in range(N_GROUPS):
        blk = sb[g * GROUP_SIZE:(g + 1) * GROUP_SIZE, :]
        m1 = jnp.max(blk, axis=0, keepdims=True)
        first = jnp.min(jnp.where(blk == m1, sub, GROUP_SIZE), axis=0, keepdims=True)
        m2 = jnp.max(jnp.where(sub == first, -jnp.inf, blk), axis=0, keepdims=True)
        gs.append(m1 + m2)
    masked = []
    for g in range(N_GROUPS):
        beat = jnp.zeros((1, tm), jnp.int32)
        for g2 in range(N_GROUPS):
            if g2 == g:
                continue
            if g2 < g:
                beat = beat + (gs[g2] >= gs[g]).astype(jnp.int32)
            else:
                beat = beat + (gs[g2] > gs[g]).astype(jnp.int32)
        keep = beat < TOPK_GROUPS
        masked.append(jnp.where(keep, sb[g * GROUP_SIZE:(g + 1) * GROUP_SIZE, :], NEG_BIG))
    m = jnp.concatenate(masked, axis=0)
    erow = lax.broadcasted_iota(jnp.int32, (N_EXPERTS, tm), 0)
    sel = jnp.zeros((N_EXPERTS, tm), jnp.bool_)
    for _ in range(TOP_K):
        mx = jnp.max(m, axis=0, keepdims=True)
        first = jnp.min(jnp.where(m == mx, erow, N_EXPERTS), axis=0, keepdims=True)
        hit = erow == first
        sel = jnp.logical_or(sel, hit)
        m = jnp.where(hit, -jnp.inf, m)
    w = jnp.where(sel, s, 0.0)
    w = w / jnp.sum(w, axis=0, keepdims=True) * ROUTE_SCALE
    sel_f = jnp.where(sel, 1.0, 0.0)
    earlier = jnp.where(lax.broadcasted_iota(jnp.int32, (tm, tm), 0)
                        < lax.broadcasted_iota(jnp.int32, (tm, tm), 1), 1.0, 0.0).astype(BF16)
    slot = jnp.dot(sel_f.astype(BF16), earlier, preferred_element_type=F32)
    fits = jnp.logical_and(sel, slot < MOE_CAP)
    zpad = jnp.zeros((WC_LANES - N_EXPERTS, tm), F32)
    wc_ref[...] = jnp.concatenate([jnp.where(fits, w, 0.0), zpad], axis=0).T
    wo_ref[...] = jnp.concatenate([jnp.where(fits, 0.0, w), zpad], axis=0).T
    pos_ref[...] = jnp.where(fits, slot, -1.0)
    cnt_ref[...] = jnp.broadcast_to(jnp.sum(sel_f, axis=1, keepdims=True), (N_EXPERTS, WC_LANES))


def _router(x, w_router, bias):
    T, D = x.shape
    tm = MOE_TILE
    assert T % tm == 0
    nt = T // tm
    return pl.pallas_call(
        _router_kernel,
        grid=(nt,),
        in_specs=[pl.BlockSpec((tm, D), lambda i: (i, 0)),
                  pl.BlockSpec((N_EXPERTS, D), lambda i: (0, 0)),
                  pl.BlockSpec((N_EXPERTS, 1), lambda i: (0, 0))],
        out_specs=[pl.BlockSpec((tm, WC_LANES), lambda i: (i, 0)),
                   pl.BlockSpec((tm, WC_LANES), lambda i: (i, 0)),
                   pl.BlockSpec((None, N_EXPERTS, tm), lambda i: (i, 0, 0)),
                   pl.BlockSpec((None, N_EXPERTS, WC_LANES), lambda i: (i, 0, 0))],
        out_shape=[jax.ShapeDtypeStruct((T, WC_LANES), F32),
                   jax.ShapeDtypeStruct((T, WC_LANES), F32),
                   jax.ShapeDtypeStruct((nt, N_EXPERTS, tm), F32),
                   jax.ShapeDtypeStruct((nt, N_EXPERTS, WC_LANES), F32)],
        compiler_params=_params("parallel"),
        name="router",
    )(x, w_router.T, bias.reshape(N_EXPERTS, 1))


def _slot_one_hot(pos_ref, e0, n, tm):
    slot = lax.broadcasted_iota(jnp.int32, (MOE_CAP, tm), 0).astype(F32)
    return jnp.concatenate([jnp.where(pos_ref[e:e + 1, :] == slot, 1.0, 0.0).astype(BF16)
                            for e in range(e0, e0 + n)], axis=0)


def _dispatch_kernel(xb_ref, wc_ref, pos_ref, xs_ref):
    tm = xb_ref.shape[0]
    wc = wc_ref[...]
    hi = wc.astype(BF16)
    lo = (wc - hi.astype(F32)).astype(BF16)
    x_aug = jnp.concatenate([xb_ref[...], hi[:, :N_EXPERTS], lo[:, :N_EXPERTS]], axis=1)
    rows = MOE_EGROUP * MOE_CAP
    for g in range(N_EXPERTS // MOE_EGROUP):
        p = _slot_one_hot(pos_ref, g * MOE_EGROUP, MOE_EGROUP, tm)
        xs_ref[g * rows:(g + 1) * rows, :] = jnp.dot(p, x_aug, preferred_element_type=F32).astype(BF16)


def _dispatch(xb, wc, pos):
    T, D = xb.shape
    nt = T // MOE_TILE
    return pl.pallas_call(
        _dispatch_kernel,
        grid=(nt,),
        in_specs=[pl.BlockSpec((MOE_TILE, D), lambda i: (i, 0)),
                  pl.BlockSpec((MOE_TILE, WC_LANES), lambda i: (i, 0)),
                  pl.BlockSpec((None, N_EXPERTS, MOE_TILE), lambda i: (i, 0, 0))],
        out_specs=pl.BlockSpec((None, N_EXPERTS * MOE_CAP, D + WC_LANES), lambda i: (i, 0, 0)),
        out_shape=jax.ShapeDtypeStruct((nt, N_EXPERTS * MOE_CAP, D + WC_LANES), BF16),
        compiler_params=_params("parallel"),
        name="moe_dispatch",
    )(xb, wc, pos)


def _swiglu_rows(x, col, wg_ref, wu_ref, wd_ref):
    hg = jnp.dot(x, wg_ref[...], preferred_element_type=F32)
    hu = jnp.dot(x, wu_ref[...], preferred_element_type=F32)
    h = hg * _sigmoid(hg) * hu
    if col is not None:
        h = h * col
    return jnp.dot(h.astype(BF16), wd_ref[...], preferred_element_type=F32)


def _experts_kernel(xs_ref, wg_ref, wu_ref, wd_ref, ys_ref):
    e = pl.program_id(0)
    tb, cap, width = xs_ref.shape
    d = width - WC_LANES
    xa = xs_ref[...].reshape(tb * cap, width)
    aug = xa[:, d:].astype(F32)
    lane = lax.broadcasted_iota(jnp.int32, aug.shape, 1)
    mine = jnp.logical_or(lane == e, lane == e + N_EXPERTS)
    col = jnp.sum(jnp.where(mine, aug, 0.0), axis=1, keepdims=True)
    y = _swiglu_rows(xa[:, :d], col, wg_ref, wu_ref, wd_ref)
    ys_ref[...] = y.astype(BF16).reshape(tb, cap, d)


def _experts(xs, wg, wu, wd):
    nt, _, width = xs.shape
    n_e, D, F = wg.shape
    tb = _pick_tile(nt, MOE_TILES_PER_STEP) if nt > MOE_TILES_PER_STEP else nt
    if nt % tb:
        tb = 1
    return pl.pallas_call(
        _experts_kernel,
        grid=(n_e, nt // tb),
        in_specs=[pl.BlockSpec((tb, MOE_CAP, width), lambda e, j: (j, e, 0)),
                  pl.BlockSpec((None, D, F), lambda e, j: (e, 0, 0)),
                  pl.BlockSpec((None, D, F), lambda e, j: (e, 0, 0)),
                  pl.BlockSpec((None, F, D), lambda e, j: (e, 0, 0))],
        out_specs=pl.BlockSpec((tb, MOE_CAP, D), lambda e, j: (j, e, 0)),
        out_shape=jax.ShapeDtypeStruct((nt, n_e * MOE_CAP, D), BF16),
        compiler_params=_params("parallel", "parallel"),
        name="moe_experts",
    )(xs, wg, wu, wd)


def _combine_kernel(ys_ref, pos_ref, x_ref, xb_ref, yo_ref, sg_ref, su_ref, sd_ref, g_ref, b_ref,
                    y_ref, yb_ref, *, alpha):
    tm = x_ref.shape[0]
    acc = yo_ref[...] + _swiglu_rows(xb_ref[...], None, sg_ref, su_ref, sd_ref)
    rows = MOE_EGROUP * MOE_CAP
    for g in range(N_EXPERTS // MOE_EGROUP):
        p = _slot_one_hot(pos_ref, g * MOE_EGROUP, MOE_EGROUP, tm)
        acc = acc + lax.dot_general(p, ys_ref[g * rows:(g + 1) * rows, :], (((0,), (0,)), ((), ())),
                                    preferred_element_type=F32)
    y = _layer_norm(alpha * x_ref[...] + acc, g_ref[...], b_ref[...])
    y_ref[...] = y
    yb_ref[...] = y.astype(BF16)


def _combine(ys, pos, x, xb, y_over, sg, su, sd, g, b, alpha):
    T, D = x.shape
    F = sg.shape[1]
    nt = T // MOE_TILE
    row = pl.BlockSpec((MOE_TILE, D), lambda i: (i, 0))
    vec = pl.BlockSpec((1, D), lambda i: (0, 0))
    return pl.pallas_call(
        functools.partial(_combine_kernel, alpha=alpha),
        grid=(nt,),
        in_specs=[pl.BlockSpec((None, N_EXPERTS * MOE_CAP, D), lambda i: (i, 0, 0)),
                  pl.BlockSpec((None, N_EXPERTS, MOE_TILE), lambda i: (i, 0, 0)),
                  row, row, row,
                  pl.BlockSpec((D, F), lambda i: (0, 0)),
                  pl.BlockSpec((D, F), lambda i: (0, 0)),
                  pl.BlockSpec((F, D), lambda i: (0, 0)),
                  vec, vec],
        out_specs=[row, row],
        out_shape=[jax.ShapeDtypeStruct((T, D), F32), jax.ShapeDtypeStruct((T, D), BF16)],
        compiler_params=_params("parallel"),
        name="moe_combine",
    )(ys, pos, x, xb, y_over, sg, su, sd, g.reshape(1, D), b.reshape(1, D))


def _dense_experts_kernel(xb_ref, wo_ref, wg_ref, wu_ref, wd_ref, y_ref):
    e = pl.program_id(1)

    @pl.when(e == 0)
    def _():
        y_ref[...] = jnp.zeros_like(y_ref)

    lane = lax.broadcasted_iota(jnp.int32, wo_ref.shape, 1)
    col = jnp.sum(jnp.where(lane == e, wo_ref[...], 0.0), axis=1, keepdims=True)
    y_ref[...] += _swiglu_rows(xb_ref[...], col, wg_ref, wu_ref, wd_ref)


def _dense_experts(xb, wo, wg, wu, wd):
    T, D = xb.shape
    n_e, _, F = wg.shape
    tm = _pick_tile(T, 1024)
    return pl.pallas_call(
        _dense_experts_kernel,
        grid=(T // tm, n_e),
        in_specs=[pl.BlockSpec((tm, D), lambda i, e: (i, 0)),
                  pl.BlockSpec((tm, WC_LANES), lambda i, e: (i, 0)),
                  pl.BlockSpec((None, D, F), lambda i, e: (e, 0, 0)),
                  pl.BlockSpec((None, D, F), lambda i, e: (e, 0, 0)),
                  pl.BlockSpec((None, F, D), lambda i, e: (e, 0, 0))],
        out_specs=pl.BlockSpec((tm, D), lambda i, e: (i, 0)),
        out_shape=jax.ShapeDtypeStruct((T, D), F32),
        compiler_params=_params("parallel", "arbitrary"),
        name="moe_dense_overflow",
    )(xb, wo, wg, wu, wd)


def _moe(xb, x, w_router, bias, wg, wu, wd, sg, su, sd, g, b, alpha):
    wc, wo, pos, cnt = _router(x, w_router, bias)
    ys = _experts(_dispatch(xb, wc, pos), wg, wu, wd)
    y_over = lax.cond(jnp.max(cnt) > MOE_CAP,
                      lambda: _dense_experts(xb, wo, wg, wu, wd),
                      lambda: jnp.zeros(x.shape, F32))
    return _combine(ys, pos, x, xb, y_over, sg, su, sd, g, b, alpha)


def kernel(x_prompt, x_sample, cache_sb_k, cache_sb_v, state_hgrn, ln_in_g, ln_in_b, w_in, w_out, hg_norm, hg_lb_logits, ln1_g, ln1_b, w_router, router_bias, w_exp_gate, w_exp_up, w_exp_down, w_sh_gate, w_sh_up, w_sh_down, ln2_g, ln2_b):
    B, S, D = x_prompt.shape
    Bs, Ts, _ = x_sample.shape
    depth = w_in.shape[0]
    P = cache_sb_k.shape[2]
    n_p = B * S
    alpha = (2 * depth) ** 0.25

    x_all = jnp.concatenate([x_prompt.reshape(n_p, D), x_sample.reshape(Bs * Ts, D)], axis=0)
    x, xb = _entry_ln(x_all, ln_in_g, ln_in_b)

    sp, sd = [], []
    kp = vp = kd = vd = None
    for l in range(depth):
        w_in_b = w_in[l].astype(BF16)
        proj = _in_proj(xb, w_in_b)
        kb_p, vb_p, kp, vp = _kv_proj(xb, w_in_b, kp, vp, l, depth, 0, n_p)
        kb_s, vb_s, kd, vd = _kv_proj(xb, w_in_b, kd, vd, l, depth, n_p, Bs * Ts)
        o_sb = _sb_prompt(proj, kb_p, vb_p, B, S)
        o_sb = _sb_sample(proj, kb_s, vb_s, cache_sb_k, cache_sb_v, o_sb, l, n_p, Bs, Ts)
        o_hg, s_p = _hgrn(proj, hg_lb_logits, hg_norm[l], None, None, l, 0, B, S)
        o_hg, s_s = _hgrn(proj, hg_lb_logits, hg_norm[l], state_hgrn[l], o_hg, l, n_p, Bs, Ts)
        x, xb = _merge_out(proj, o_sb, o_hg, x, w_out[l].astype(BF16), ln1_g[l], ln1_b[l], alpha)
        x, xb = _moe(xb, x, w_router[l], router_bias[l], w_exp_gate[l].astype(BF16), w_exp_up[l].astype(BF16),
                     w_exp_down[l].astype(BF16), w_sh_gate[l].astype(BF16), w_sh_up[l].astype(BF16),
                     w_sh_down[l].astype(BF16), ln2_g[l], ln2_b[l], alpha)
        sp.append(s_p)
        sd.append(s_s)
    kv_p = (depth, B, S, N_HEADS, HEAD_DIM)
    kv_s = (depth, Bs, Ts, N_HEADS, HEAD_DIM)
    return (x[:n_p].reshape(B, S, D), x[n_p:].reshape(Bs, Ts, D),
            kp.reshape(kv_p), vp.reshape(kv_p), jnp.stack(sp),
            kd.reshape(kv_s), vd.reshape(kv_s), jnp.stack(sd))
```

```python
import functools
import math

import jax
import jax.numpy as jnp
from jax import lax
from jax.experimental import pallas as pl
from jax.experimental.pallas import tpu as pltpu

F32 = jnp.float32
BF16 = jnp.bfloat16

N_HEADS = 8
HEAD_DIM = 128
N_EXPERTS = 64
TOP_K = 8
N_GROUPS = 8
TOPK_GROUPS = 4
GROUP_SIZE = N_EXPERTS // N_GROUPS
ROUTE_SCALE = 2.5
NEG_BIG = -1e9
LB_FLOOR = 1e-30
LN_EPS = 1e-5
RMS_EPS = 1e-6
LOG2E = 1.4426950408889634
N_SPLITS = 9
P_Q, P_HQ, P_HF, P_HI, P_HG, P_GA, P_GB = range(7)

LANES = 128
SUBLANES = 8
VMEM_LIMIT = 48 * 1024 * 1024

HG_CHUNK = 64
HG_SUB = 32
SB_BLOCK = 256
SB_GROUP = 8
WC_LANES = 128
MOE_TILE = 256
MOE_CAP = 64
MOE_EGROUP = 8
MOE_TILES_PER_STEP = 24


def _pick_tile(n, pref):
    if n <= pref:
        return n
    for t in range(pref, 7, -1):
        if n % t == 0 and t % SUBLANES == 0:
            return t
    raise ValueError(f"no tile for {n}")


def _params(*sem):
    return pltpu.CompilerParams(dimension_semantics=sem, vmem_limit_bytes=VMEM_LIMIT)


def _layer_norm(x, g, b):
    mu = jnp.mean(x, axis=-1, keepdims=True)
    xc = x - mu
    var = jnp.mean(xc * xc, axis=-1, keepdims=True)
    return xc * lax.rsqrt(var + LN_EPS) * g + b


def _sigmoid(x):
    return 1.0 / (1.0 + jnp.exp(-x))


def _log_sigmoid(x):
    return jnp.minimum(x, 0.0) - jnp.log(1.0 + jnp.exp(-jnp.abs(x)))


def _ln_kernel(x_ref, g_ref, b_ref, y_ref, yb_ref):
    y = _layer_norm(x_ref[...], g_ref[...], b_ref[...])
    y_ref[...] = y
    yb_ref[...] = y.astype(BF16)


def _entry_ln(x, g, b):
    T, D = x.shape
    tm = _pick_tile(T, 512)
    return pl.pallas_call(
        _ln_kernel,
        grid=(T // tm,),
        in_specs=[pl.BlockSpec((tm, D), lambda i: (i, 0)),
                  pl.BlockSpec((1, D), lambda i: (0, 0)),
                  pl.BlockSpec((1, D), lambda i: (0, 0))],
        out_specs=[pl.BlockSpec((tm, D), lambda i: (i, 0)),
                   pl.BlockSpec((tm, D), lambda i: (i, 0))],
        out_shape=[jax.ShapeDtypeStruct((T, D), F32), jax.ShapeDtypeStruct((T, D), BF16)],
        compiler_params=_params("parallel"),
        name="entry_ln",
    )(x, g.reshape(1, D), b.reshape(1, D))


def _in_proj_kernel(x_ref, w_ref, o_ref):
    o_ref[...] = jnp.dot(x_ref[...], w_ref[...], preferred_element_type=F32)


def _in_proj(xb, w_in_b):
    T, D = xb.shape
    tm = _pick_tile(T, 1024)
    return pl.pallas_call(
        _in_proj_kernel,
        grid=(N_SPLITS - 2, T // tm),
        in_specs=[pl.BlockSpec((tm, D), lambda j, i: (i, 0)),
                  pl.BlockSpec((D, D), lambda j, i: (0, jnp.where(j >= 1, j + 2, j)))],
        out_specs=pl.BlockSpec((None, tm, D), lambda j, i: (j, i, 0)),
        out_shape=jax.ShapeDtypeStruct((N_SPLITS - 2, T, D), F32),
        compiler_params=_params("parallel", "parallel"),
        name="in_proj",
    )(xb, w_in_b)


def _kv_proj_kernel(x_ref, wk_ref, wv_ref, *refs):
    kb_ref, vb_ref, ko_ref, vo_ref = refs[-4:]
    tm = x_ref.shape[0]
    x = x_ref[...]
    for w_ref, b_ref, o_ref in ((wk_ref, kb_ref, ko_ref), (wv_ref, vb_ref, vo_ref)):
        y = jnp.dot(x, w_ref[...], preferred_element_type=F32)
        b_ref[...] = y.astype(BF16)
        for h in range(N_HEADS):
            o_ref[pl.ds(h, tm, stride=N_HEADS), :] = y[:, h * HEAD_DIM:(h + 1) * HEAD_DIM]


def _kv_proj(xb, w_in_b, k_prev, v_prev, layer, depth, row0, n):
    D = xb.shape[1]
    tm = _pick_tile(n, 512)
    assert row0 % tm == 0
    rb = row0 // tm
    out_spec = pl.BlockSpec((None, tm * N_HEADS, HEAD_DIM), lambda i: (layer, i, 0))
    out_sds = jax.ShapeDtypeStruct((depth, n * N_HEADS, HEAD_DIM), F32)
    args = [xb, w_in_b, w_in_b]
    in_specs = [pl.BlockSpec((tm, D), lambda i: (rb + i, 0)),
                pl.BlockSpec((D, D), lambda i: (0, 1)),
                pl.BlockSpec((D, D), lambda i: (0, 2))]
    aliases = {}
    if k_prev is not None:
        args += [k_prev, v_prev]
        in_specs += [pl.BlockSpec(memory_space=pl.ANY)] * 2
        aliases = {3: 2, 4: 3}
    return pl.pallas_call(
        _kv_proj_kernel,
        grid=(n // tm,),
        in_specs=in_specs,
        out_specs=[pl.BlockSpec((tm, D), lambda i: (i, 0)), pl.BlockSpec((tm, D), lambda i: (i, 0)),
                   out_spec, out_spec],
        out_shape=[jax.ShapeDtypeStruct((n, D), BF16), jax.ShapeDtypeStruct((n, D), BF16),
                   out_sds, out_sds],
        input_output_aliases=aliases,
        compiler_params=_params("parallel"),
        name="kv_proj",
    )(*args)


def _neg_tri(tk):
    r = lax.broadcasted_iota(jnp.int32, (tk, tk), 0)
    c = lax.broadcasted_iota(jnp.int32, (tk, tk), 1)
    return jnp.where(r >= c, -1.0, 0.0).astype(BF16)


def _strict_lower(tq, tk):
    return (lax.broadcasted_iota(jnp.int32, (tq, tk), 1)
            < lax.broadcasted_iota(jnp.int32, (tq, tk), 0))


def _sb_z(qs, kb):
    return lax.dot_general(qs, kb, (((1,), (1,)), ((), ())), preferred_element_type=F32)


def _sb_cumsum_lhs(z, tk, mask):
    sp = jnp.maximum(z, 0.0) + jnp.log(1.0 + jnp.exp2(jnp.abs(z) * (-LOG2E)))
    if mask is not None:
        sp = jnp.where(mask, sp, 0.0)
    sp = sp.astype(BF16)
    n = z.shape[1] // tk
    return sp if n == 1 else jnp.concatenate([sp[:, a * tk:(a + 1) * tk] for a in range(n)], axis=0)


def _sb_weights(z, c, carry, tk, mask):
    tq = z.shape[0]
    n = z.shape[1] // tk
    ws = [None] * n
    for a in range(n - 1, -1, -1):
        ca = c[a * tq:(a + 1) * tq, :]
        w = jnp.exp2((z[:, a * tk:(a + 1) * tk] + ca + carry) * LOG2E)
        if mask is not None:
            w = jnp.where(mask, w, 0.0)
        ws[a] = w.astype(BF16)
        carry = carry + ca[:, 0:1]
    return (ws[0] if n == 1 else jnp.concatenate(ws, axis=1)), carry


def _sb_prompt_kernel(q_ref, kb_scr, vb_scr, o_ref, tri_scr, *, blk):
    i = pl.program_id(2)

    @pl.when(i == 0)
    def _():
        tri_scr[...] = _neg_tri(blk)

    qs = (q_ref[...] * (1.0 / math.sqrt(HEAD_DIM))).astype(BF16)

    def block_rows(j):
        return pl.ds(pl.multiple_of(j * blk, blk), blk)

    def sweep(j_lo, n, co, diag_last, zs=None):
        carry, o = co
        rows = [block_rows(j_lo + a) for a in range(n)]
        masks = [_strict_lower(blk, blk) if diag_last and a == n - 1 else None for a in range(n)]
        if zs is None:
            zs = [_sb_z(qs, kb_scr[r, :]) for r in rows]
        cs = [jnp.dot(_sb_cumsum_lhs(z, blk, m), tri_scr[...], preferred_element_type=F32)
              for z, m in zip(zs, masks)]
        for a in range(n - 1, -1, -1):
            w, carry = _sb_weights(zs[a], cs[a], carry, blk, masks[a])
            o = o + jnp.dot(w, vb_scr[rows[a], :], preferred_element_type=F32)
        return carry, o

    co = (jnp.zeros((blk, 1), F32), jnp.zeros((blk, HEAD_DIM), F32))
    rem = i % SB_GROUP
    co = lax.switch(rem, [functools.partial(lambda r, co: sweep(i - r, r + 1, co, True), r)
                          for r in range(SB_GROUP)], co)
    co = lax.fori_loop(0, i // SB_GROUP,
                       lambda p, co: sweep(i - rem - SB_GROUP * (p + 1), SB_GROUP, co, False), co)
    o_ref[...] = co[1]


def _sb_prompt(proj, kb, vb, B, S):
    D = N_HEADS * HEAD_DIM
    t_all = proj.shape[1]
    blk = _pick_tile(S, SB_BLOCK)
    nq = S // blk
    return pl.pallas_call(
        functools.partial(_sb_prompt_kernel, blk=blk),
        grid=(B, N_HEADS, nq),
        in_specs=[pl.BlockSpec((None, blk, HEAD_DIM), lambda b, h, i: (P_Q, b * nq + i, h)),
                  pl.BlockSpec((S, HEAD_DIM), lambda b, h, i: (b, h)),
                  pl.BlockSpec((S, HEAD_DIM), lambda b, h, i: (b, h))],
        out_specs=pl.BlockSpec((blk, HEAD_DIM), lambda b, h, i: (b * nq + i, h)),
        out_shape=jax.ShapeDtypeStruct((t_all, D), F32),
        scratch_shapes=[pltpu.VMEM((blk, blk), BF16)],
        compiler_params=_params("parallel", "parallel", "arbitrary"),
        name="sb_prompt",
    )(proj, kb, vb)


def _sb_sample_kernel(q_ref, k_ref, v_ref, pk_ref, pv_ref, o_in_ref, o_ref, *, pblk):
    del o_in_ref
    T = q_ref.shape[0]
    P = pk_ref.shape[0] // N_HEADS

    def head_rows(ref, h):
        return ref[pl.ds(h, P, stride=N_HEADS), :].astype(BF16)

    heads = range(N_HEADS)
    cols = [slice(h * HEAD_DIM, (h + 1) * HEAD_DIM) for h in heads]
    mask = _strict_lower(T, T)
    qs = [(q_ref[:, cols[h]] * (1.0 / math.sqrt(HEAD_DIM))).astype(BF16) for h in heads]
    z_new = [_sb_z(qs[h], k_ref[:, cols[h]]) for h in heads]
    z_past = [_sb_z(qs[h], head_rows(pk_ref, h)) for h in heads]
    lhs_new = jnp.concatenate([_sb_cumsum_lhs(z_new[h], T, mask) for h in heads], axis=0)
    lhs_past = jnp.concatenate([_sb_cumsum_lhs(z_past[h], pblk, None) for h in heads], axis=0)
    c_new = jnp.dot(lhs_new, _neg_tri(T), preferred_element_type=F32)
    c_past = jnp.dot(lhs_past, _neg_tri(pblk), preferred_element_type=F32)
    rows_past = (P // pblk) * T
    for h in heads:
        w_new, carry = _sb_weights(z_new[h], c_new[h * T:(h + 1) * T], jnp.zeros((T, 1), F32), T, mask)
        w_past, _ = _sb_weights(z_past[h], c_past[h * rows_past:(h + 1) * rows_past], carry, pblk, None)
        o_ref[:, cols[h]] = (jnp.dot(w_new, v_ref[:, cols[h]], preferred_element_type=F32)
                             + jnp.dot(w_past, head_rows(pv_ref, h), preferred_element_type=F32))


def _sb_sample(proj, kb, vb, cache_k, cache_v, o_sb, layer, row0, B, T):
    D = N_HEADS * HEAD_DIM
    depth, _, P = cache_k.shape[:3]
    pblk = _pick_tile(P, SB_BLOCK)
    rb = row0 // T
    cache_k = cache_k.reshape(depth, B, P * N_HEADS, HEAD_DIM)
    cache_v = cache_v.reshape(depth, B, P * N_HEADS, HEAD_DIM)
    cache_spec = pl.BlockSpec((None, None, P * N_HEADS, HEAD_DIM), lambda b: (layer, b, 0, 0))
    return pl.pallas_call(
        functools.partial(_sb_sample_kernel, pblk=pblk),
        grid=(B,),
        in_specs=[pl.BlockSpec((None, T, D), lambda b: (P_Q, rb + b, 0)),
                  pl.BlockSpec((T, D), lambda b: (b, 0)),
                  pl.BlockSpec((T, D), lambda b: (b, 0)),
                  cache_spec, cache_spec,
                  pl.BlockSpec(memory_space=pl.ANY)],
        out_specs=pl.BlockSpec((T, D), lambda b: (rb + b, 0)),
        out_shape=jax.ShapeDtypeStruct(o_sb.shape, F32),
        input_output_aliases={5: 0},
        compiler_params=_params("parallel"),
        name="sb_sample",
    )(proj, kb, vb, cache_k, cache_v, o_sb)


def _cumsum_rows(x):
    n = x.shape[0]
    row = lax.broadcasted_iota(jnp.int32, x.shape, 0)
    s = 1
    while s < n:
        x = x + jnp.where(row >= s, pltpu.roll(x, s, 0), 0.0)
        s *= 2
    return x


def _hgrn_kernel(*refs, layer, has_s0):
    if has_s0:
        q_ref, f_ref, i_ref, g_ref, lbl_ref, norm_ref, s0_ref, _, o_ref, sf_ref, st_scr = refs
    else:
        q_ref, f_ref, i_ref, g_ref, lbl_ref, norm_ref, o_ref, sf_ref, st_scr = refs
    c = pl.program_id(1)
    C = q_ref.shape[0]
    n_sub = C // HG_SUB

    @pl.when(c == 0)
    def _():
        for h in range(N_HEADS):
            if has_s0:
                st_scr[h] = s0_ref[h].T
            else:
                st_scr[h] = jnp.zeros((HEAD_DIM, HEAD_DIM), F32)

    lg = lbl_ref[...]
    ex = jnp.exp(lg - jnp.max(lg, axis=0, keepdims=True))
    soft = ex / jnp.sum(ex, axis=0, keepdims=True)
    lb = jnp.zeros((1, lg.shape[1]), F32)
    for m in range(1, layer + 1):
        lb = lb + soft[m:m + 1, :]

    a = f_ref[...]
    x1 = _log_sigmoid(a)
    x2 = _log_sigmoid(-a) + jnp.log(jnp.maximum(lb, LB_FLOOR))
    logf = jnp.maximum(x1, x2) + jnp.log(1.0 + jnp.exp(-jnp.abs(x1 - x2)))
    logf = jnp.minimum(logf, 0.0)
    kk_all = 1.0 - jnp.exp(logf)
    b_all = _cumsum_rows(logf)
    qraw = q_ref[...]
    q_all = qraw * _sigmoid(qraw)
    graw = g_ref[...]
    gate_all = graw * _sigmoid(graw)
    v_all = i_ref[...]

    row = lax.broadcasted_iota(jnp.int32, (C, HEAD_DIM), 0)
    for h in range(N_HEADS):
        hs = slice(h * HEAD_DIM, (h + 1) * HEAD_DIM)
        b = b_all[:, hs]
        q = q_all[:, hs]
        kk = kk_all[:, hs]
        v = v_all[:, hs]
        vb = v.astype(BF16)
        st = st_scr[h]
        qt = (q * jnp.exp(b)).astype(BF16)
        o = lax.dot_general(qt, st.astype(BF16), (((1,), (1,)), ((), ())),
                            preferred_element_type=F32)
        parts = []
        for i in range(n_sub):
            lo_r, hi_r = i * HG_SUB, (i + 1) * HG_SUB
            if i == 0:
                ref_row = jnp.zeros((1, HEAD_DIM), F32)
            else:
                ref_row = b[lo_r - 1:lo_r, :]
            qh = (q[lo_r:hi_r] * jnp.exp(b[lo_r:hi_r] - ref_row)).astype(BF16)
            kh = (kk[0:hi_r] * jnp.exp(ref_row - b[0:hi_r])).astype(BF16)
            sc = lax.dot_general(qh, kh, (((1,), (1,)), ((), ())),
                                 preferred_element_type=F32)
            tpos = lax.broadcasted_iota(jnp.int32, (HG_SUB, hi_r), 0) + lo_r
            spos = lax.broadcasted_iota(jnp.int32, (HG_SUB, hi_r), 1)
            sc = jnp.where(spos <= tpos, sc, 0.0)
            parts.append(jnp.dot(sc.astype(BF16), vb[0:hi_r], preferred_element_type=F32))
        o = o + jnp.concatenate(parts, axis=0)
        b_last = b[C - 1:C, :]
        ke = (kk * jnp.exp(b_last - b)).astype(BF16)
        upd = lax.dot_general(vb, ke, (((0,), (0,)), ((), ())), preferred_element_type=F32)
        st_scr[h] = st * jnp.exp(b_last) + upd
        ms = jnp.mean(o * o, axis=-1, keepdims=True)
        o_ref[:, hs] = o * lax.rsqrt(ms + RMS_EPS) * norm_ref[...] * gate_all[:, hs]

    @pl.when(c == pl.num_programs(1) - 1)
    def _():
        for h in range(N_HEADS):
            sf_ref[h] = st_scr[h].T


def _hgrn(proj, lb_logits, hg_norm, s0, o_prev, layer, row0, B, T):
    D = N_HEADS * HEAD_DIM
    C = HG_CHUNK
    assert T % C == 0 and row0 % C == 0
    nc = T // C
    rb = row0 // C
    depth = lb_logits.shape[0]
    t_all = proj.shape[1]

    def act_spec(split):
        return pl.BlockSpec((None, C, D), lambda b, c: (split, rb + b * nc + c, 0))

    in_specs = [act_spec(P_HQ), act_spec(P_HF), act_spec(P_HI), act_spec(P_HG),
                pl.BlockSpec((depth, D), lambda b, c: (0, 0)),
                pl.BlockSpec((1, HEAD_DIM), lambda b, c: (0, 0))]
    args = [proj, proj, proj, proj, lb_logits, hg_norm.reshape(1, HEAD_DIM)]
    if s0 is not None:
        in_specs.append(pl.BlockSpec((None, N_HEADS, HEAD_DIM, HEAD_DIM), lambda b, c: (b, 0, 0, 0)))
        in_specs.append(pl.BlockSpec(memory_space=pl.ANY))
        args += [s0, o_prev]
    return pl.pallas_call(
        functools.partial(_hgrn_kernel, layer=layer, has_s0=s0 is not None),
        grid=(B, nc),
        in_specs=in_specs,
        out_specs=[pl.BlockSpec((C, D), lambda b, c: (rb + b * nc + c, 0)),
                   pl.BlockSpec((None, N_HEADS, HEAD_DIM, HEAD_DIM), lambda b, c: (b, 0, 0, 0))],
        out_shape=[jax.ShapeDtypeStruct((t_all, D), F32),
                   jax.ShapeDtypeStruct((B, N_HEADS, HEAD_DIM, HEAD_DIM), F32)],
        input_output_aliases={} if s0 is None else {len(args) - 1: 0},
        scratch_shapes=[pltpu.VMEM((N_HEADS, HEAD_DIM, HEAD_DIM), F32)],
        compiler_params=_params("parallel", "arbitrary"),
        name="hgrn2",
    )(*args)


def _merge_out_kernel(ga_ref, gb_ref, oa_ref, ob_ref, x_ref, w_ref, g_ref, b_ref, y_ref, yb_ref, *, alpha):
    merged = _sigmoid(ga_ref[...]) * oa_ref[...] + _sigmoid(gb_ref[...]) * ob_ref[...]
    r = alpha * x_ref[...] + jnp.dot(merged.astype(BF16), w_ref[...], preferred_element_type=F32)
    y = _layer_norm(r, g_ref[...], b_ref[...])
    y_ref[...] = y
    yb_ref[...] = y.astype(BF16)


def _merge_out(proj, o_sb, o_hg, x, w_out_b, g, b, alpha):
    T, D = x.shape
    tm = _pick_tile(T, 256)
    row = pl.BlockSpec((tm, D), lambda i: (i, 0))
    vec = pl.BlockSpec((1, D), lambda i: (0, 0))
    return pl.pallas_call(
        functools.partial(_merge_out_kernel, alpha=alpha),
        grid=(T // tm,),
        in_specs=[pl.BlockSpec((None, tm, D), lambda i: (P_GA, i, 0)),
                  pl.BlockSpec((None, tm, D), lambda i: (P_GB, i, 0)),
                  row, row, row,
                  pl.BlockSpec((D, D), lambda i: (0, 0)), vec, vec],
        out_specs=[row, row],
        out_shape=[jax.ShapeDtypeStruct((T, D), F32), jax.ShapeDtypeStruct((T, D), BF16)],
        compiler_params=_params("parallel"),
        name="merge_out",
    )(proj, proj, o_sb, o_hg, x, w_out_b, g.reshape(1, D), b.reshape(1, D))


def _router_kernel(x_ref, wr_ref, bias_ref, wc_ref, wo_ref, pos_ref, cnt_ref):
    tm = x_ref.shape[0]
    logits = lax.dot_general(wr_ref[...], x_ref[...], (((1,), (1,)), ((), ())),
                             precision=lax.Precision.HIGHEST, preferred_element_type=F32)
    s = _sigmoid(logits)
    sb = s + bias_ref[...]
    sub = lax.broadcasted_iota(jnp.int32, (GROUP_SIZE, tm), 0)
    gs = []
    for g in range(N_GROUPS):
        blk = sb[g * GROUP_SIZE:(g + 1) * GROUP_SIZE, :]
        m1 = jnp.max(blk, axis=0, keepdims=True)
        first = jnp.min(jnp.where(blk == m1, sub, GROUP_SIZE), axis=0, keepdims=True)
        m2 = jnp.max(jnp.where(sub == first, -jnp.inf, blk), axis=0, keepdims=True)
        gs.append(m1 + m2)
    masked = []
    for g in range(N_GROUPS):
        beat = jnp.zeros((1, tm), jnp.int32)
        for g2 in range(N_GROUPS):
            if g2 == g:
                continue
            if g2 < g:
                beat = beat + (gs[g2] >= gs[g]).astype(jnp.int32)
            else:
                beat = beat + (gs[g2] > gs[g]).astype(jnp.int32)
        keep = beat < TOPK_GROUPS
        masked.append(jnp.where(keep, sb[g * GROUP_SIZE:(g + 1) * GROUP_SIZE, :], NEG_BIG))
    m = jnp.concatenate(masked, axis=0)
    erow = lax.broadcasted_iota(jnp.int32, (N_EXPERTS, tm), 0)
    sel = jnp.zeros((N_EXPERTS, tm), jnp.bool_)
    for _ in range(TOP_K):
        mx = jnp.max(m, axis=0, keepdims=True)
        first = jnp.min(jnp.where(m == mx, erow, N_EXPERTS), axis=0, keepdims=True)
        hit = erow == first
        sel = jnp.logical_or(sel, hit)
        m = jnp.where(hit, -jnp.inf, m)
    w = jnp.where(sel, s, 0.0)
    w = w / jnp.sum(w, axis=0, keepdims=True) * ROUTE_SCALE
    sel_f = jnp.where(sel, 1.0, 0.0)
    earlier = jnp.where(lax.broadcasted_iota(jnp.int32, (tm, tm), 0)
                        < lax.broadcasted_iota(jnp.int32, (tm, tm), 1), 1.0, 0.0).astype(BF16)
    slot = jnp.dot(sel_f.astype(BF16), earlier, preferred_element_type=F32)
    fits = jnp.logical_and(sel, slot < MOE_CAP)
    zpad = jnp.zeros((WC_LANES - N_EXPERTS, tm), F32)
    wc_ref[...] = jnp.concatenate([jnp.where(fits, w, 0.0), zpad], axis=0).T
    wo_ref[...] = jnp.concatenate([jnp.where(fits, 0.0, w), zpad], axis=0).T
    pos_ref[...] = jnp.where(fits, slot, -1.0)
    cnt_ref[...] = jnp.broadcast_to(jnp.sum(sel_f, axis=1, keepdims=True), (N_EXPERTS, WC_LANES))


def _router(x, w_router, bias):
    T, D = x.shape
    tm = MOE_TILE
    assert T % tm == 0
    nt = T // tm
    return pl.pallas_call(
        _router_kernel,
        grid=(nt,),
        in_specs=[pl.BlockSpec((tm, D), lambda i: (i, 0)),
                  pl.BlockSpec((N_EXPERTS, D), lambda i: (0, 0)),
                  pl.BlockSpec((N_EXPERTS, 1), lambda i: (0, 0))],
        out_specs=[pl.BlockSpec((tm, WC_LANES), lambda i: (i, 0)),
                   pl.BlockSpec((tm, WC_LANES), lambda i: (i, 0)),
                   pl.BlockSpec((None, N_EXPERTS, tm), lambda i: (i, 0, 0)),
                   pl.BlockSpec((None, N_EXPERTS, WC_LANES), lambda i: (i, 0, 0))],
        out_shape=[jax.ShapeDtypeStruct((T, WC_LANES), F32),
                   jax.ShapeDtypeStruct((T, WC_LANES), F32),
                   jax.ShapeDtypeStruct((nt, N_EXPERTS, tm), F32),
                   jax.ShapeDtypeStruct((nt, N_EXPERTS, WC_LANES), F32)],
        compiler_params=_params("parallel"),
        name="router",
    )(x, w_router.T, bias.reshape(N_EXPERTS, 1))


def _slot_one_hot(pos_ref, e0, n, tm):
    slot = lax.broadcasted_iota(jnp.int32, (MOE_CAP, tm), 0).astype(F32)
    return jnp.concatenate([jnp.where(pos_ref[e:e + 1, :] == slot, 1.0, 0.0).astype(BF16)
                            for e in range(e0, e0 + n)], axis=0)


def _dispatch_kernel(xb_ref, wc_ref, pos_ref, xs_ref):
    tm = xb_ref.shape[0]
    wc = wc_ref[...]
    hi = wc.astype(BF16)
    lo = (wc - hi.astype(F32)).astype(BF16)
    x_aug = jnp.concatenate([xb_ref[...], hi[:, :N_EXPERTS], lo[:, :N_EXPERTS]], axis=1)
    rows = MOE_EGROUP * MOE_CAP
    for g in range(N_EXPERTS // MOE_EGROUP):
        p = _slot_one_hot(pos_ref, g * MOE_EGROUP, MOE_EGROUP, tm)
        xs_ref[g * rows:(g + 1) * rows, :] = jnp.dot(p, x_aug, preferred_element_type=F32).astype(BF16)


def _dispatch(xb, wc, pos):
    T, D = xb.shape
    nt = T // MOE_TILE
    return pl.pallas_call(
        _dispatch_kernel,
        grid=(nt,),
        in_specs=[pl.BlockSpec((MOE_TILE, D), lambda i: (i, 0)),
                  pl.BlockSpec((MOE_TILE, WC_LANES), lambda i: (i, 0)),
                  pl.BlockSpec((None, N_EXPERTS, MOE_TILE), lambda i: (i, 0, 0))],
        out_specs=pl.BlockSpec((None, N_EXPERTS * MOE_CAP, D + WC_LANES), lambda i: (i, 0, 0)),
        out_shape=jax.ShapeDtypeStruct((nt, N_EXPERTS * MOE_CAP, D + WC_LANES), BF16),
        compiler_params=_params("parallel"),
        name="moe_dispatch",
    )(xb, wc, pos)


def _swiglu_rows(x, col, wg_ref, wu_ref, wd_ref):
    hg = jnp.dot(x, wg_ref[...].astype(BF16), preferred_element_type=F32)
    hu = jnp.dot(x, wu_ref[...].astype(BF16), preferred_element_type=F32)
    h = hg * _sigmoid(hg) * hu
    if col is not None:
        h = h * col
    return jnp.dot(h.astype(BF16), wd_ref[...].astype(BF16), preferred_element_type=F32)


def _experts_kernel(xs_ref, wg32_ref, wu32_ref, wd32_ref, ys_ref, wg_ref, wu_ref, wd_ref):
    e = pl.program_id(0)

    @pl.when(pl.program_id(1) == 0)
    def _():
        wg_ref[...] = wg32_ref[...].astype(BF16)
        wu_ref[...] = wu32_ref[...].astype(BF16)
        wd_ref[...] = wd32_ref[...].astype(BF16)

    tb, cap, width = xs_ref.shape
    d = width - WC_LANES
    xa = xs_ref[...].reshape(tb * cap, width)
    aug = xa[:, d:].astype(F32)
    lane = lax.broadcasted_iota(jnp.int32, aug.shape, 1)
    mine = jnp.logical_or(lane == e, lane == e + N_EXPERTS)
    col = jnp.sum(jnp.where(mine, aug, 0.0), axis=1, keepdims=True)
    y = _swiglu_rows(xa[:, :d], col, wg_ref, wu_ref, wd_ref)
    ys_ref[...] = y.astype(BF16).reshape(tb, cap, d)


def _experts(xs, wg, wu, wd, layer):
    nt, _, width = xs.shape
    _, n_e, D, F = wg.shape
    tb = _pick_tile(nt, MOE_TILES_PER_STEP) if nt > MOE_TILES_PER_STEP else nt
    if nt % tb:
        tb = 1
    return pl.pallas_call(
        _experts_kernel,
        grid=(n_e, nt // tb),
        in_specs=[pl.BlockSpec((tb, MOE_CAP, width), lambda e, j: (j, e, 0)),
                  pl.BlockSpec((None, None, D, F), lambda e, j: (layer, e, 0, 0)),
                  pl.BlockSpec((None, None, D, F), lambda e, j: (layer, e, 0, 0)),
                  pl.BlockSpec((None, None, F, D), lambda e, j: (layer, e, 0, 0))],
        out_specs=pl.BlockSpec((tb, MOE_CAP, D), lambda e, j: (j, e, 0)),
        out_shape=jax.ShapeDtypeStruct((nt, n_e * MOE_CAP, D), BF16),
        scratch_shapes=[pltpu.VMEM((D, F), BF16), pltpu.VMEM((D, F), BF16), pltpu.VMEM((F, D), BF16)],
        compiler_params=_params("parallel", "arbitrary"),
        name="moe_experts",
    )(xs, wg, wu, wd)


def _combine_kernel(ys_ref, pos_ref, x_ref, xb_ref, yo_ref, sg_ref, su_ref, sd_ref, g_ref, b_ref,
                    y_ref, yb_ref, *, alpha):
    tm = x_ref.shape[0]
    acc = yo_ref[...] + _swiglu_rows(xb_ref[...], None, sg_ref, su_ref, sd_ref)
    rows = MOE_EGROUP * MOE_CAP
    for g in range(N_EXPERTS // MOE_EGROUP):
        p = _slot_one_hot(pos_ref, g * MOE_EGROUP, MOE_EGROUP, tm)
        acc = acc + lax.dot_general(p, ys_ref[g * rows:(g + 1) * rows, :], (((0,), (0,)), ((), ())),
                                    preferred_element_type=F32)
    y = _layer_norm(alpha * x_ref[...] + acc, g_ref[...], b_ref[...])
    y_ref[...] = y
    yb_ref[...] = y.astype(BF16)


def _combine(ys, pos, x, xb, y_over, sg, su, sd, g, b, alpha, layer, row0, n):
    D = x.shape[1]
    F = sg.shape[2]
    assert row0 % MOE_TILE == 0 and n % MOE_TILE == 0
    t0 = row0 // MOE_TILE
    row_in = pl.BlockSpec((MOE_TILE, D), lambda i: (t0 + i, 0))
    row_out = pl.BlockSpec((MOE_TILE, D), lambda i: (i, 0))
    vec = pl.BlockSpec((1, D), lambda i: (0, 0))
    return pl.pallas_call(
        functools.partial(_combine_kernel, alpha=alpha),
        grid=(n // MOE_TILE,),
        in_specs=[pl.BlockSpec((None, N_EXPERTS * MOE_CAP, D), lambda i: (t0 + i, 0, 0)),
                  pl.BlockSpec((None, N_EXPERTS, MOE_TILE), lambda i: (t0 + i, 0, 0)),
                  row_in, row_in, row_in,
                  pl.BlockSpec((None, D, F), lambda i: (layer, 0, 0)),
                  pl.BlockSpec((None, D, F), lambda i: (layer, 0, 0)),
                  pl.BlockSpec((None, F, D), lambda i: (layer, 0, 0)),
                  vec, vec],
        out_specs=[row_out, row_out],
        out_shape=[jax.ShapeDtypeStruct((n, D), F32), jax.ShapeDtypeStruct((n, D), BF16)],
        compiler_params=_params("parallel"),
        name="moe_combine",
    )(ys, pos, x, xb, y_over, sg, su, sd, g.reshape(1, D), b.reshape(1, D))


def _dense_experts_kernel(xb_ref, wo_ref, wg_ref, wu_ref, wd_ref, y_ref):
    e = pl.program_id(1)

    @pl.when(e == 0)
    def _():
        y_ref[...] = jnp.zeros_like(y_ref)

    lane = lax.broadcasted_iota(jnp.int32, wo_ref.shape, 1)
    col = jnp.sum(jnp.where(lane == e, wo_ref[...], 0.0), axis=1, keepdims=True)
    y_ref[...] += _swiglu_rows(xb_ref[...], col, wg_ref, wu_ref, wd_ref)


def _dense_experts(xb, wo, wg, wu, wd, layer):
    T, D = xb.shape
    _, n_e, _, F = wg.shape
    tm = _pick_tile(T, 1024)
    return pl.pallas_call(
        _dense_experts_kernel,
        grid=(T // tm, n_e),
        in_specs=[pl.BlockSpec((tm, D), lambda i, e: (i, 0)),
                  pl.BlockSpec((tm, WC_LANES), lambda i, e: (i, 0)),
                  pl.BlockSpec((None, None, D, F), lambda i, e: (layer, e, 0, 0)),
                  pl.BlockSpec((None, None, D, F), lambda i, e: (layer, e, 0, 0)),
                  pl.BlockSpec((None, None, F, D), lambda i, e: (layer, e, 0, 0))],
        out_specs=pl.BlockSpec((tm, D), lambda i, e: (i, 0)),
        out_shape=jax.ShapeDtypeStruct((T, D), F32),
        compiler_params=_params("parallel", "arbitrary"),
        name="moe_dense_overflow",
    )(xb, wo, wg, wu, wd)


def _moe(xb, x, w_router, bias, wg, wu, wd, sg, su, sd, g, b, alpha, layer, ranges):
    wc, wo, pos, cnt = _router(x, w_router, bias)
    ys = _experts(_dispatch(xb, wc, pos), wg, wu, wd, layer)
    y_over = lax.cond(jnp.max(cnt) > MOE_CAP,
                      lambda: _dense_experts(xb, wo, wg, wu, wd, layer),
                      lambda: jnp.zeros(x.shape, F32))
    return [_combine(ys, pos, x, xb, y_over, sg, su, sd, g, b, alpha, layer, row0, n)
            for row0, n in ranges]


def kernel(x_prompt, x_sample, cache_sb_k, cache_sb_v, state_hgrn, ln_in_g, ln_in_b, w_in, w_out, hg_norm, hg_lb_logits, ln1_g, ln1_b, w_router, router_bias, w_exp_gate, w_exp_up, w_exp_down, w_sh_gate, w_sh_up, w_sh_down, ln2_g, ln2_b):
    B, S, D = x_prompt.shape
    Bs, Ts, _ = x_sample.shape
    depth = w_in.shape[0]
    P = cache_sb_k.shape[2]
    n_p = B * S
    alpha = (2 * depth) ** 0.25

    x_all = jnp.concatenate([x_prompt.reshape(n_p, D), x_sample.reshape(Bs * Ts, D)], axis=0)
    x, xb = _entry_ln(x_all, ln_in_g, ln_in_b)

    sp, sd = [], []
    kp = vp = kd = vd = None
    for l in range(depth):
        w_in_b = w_in[l].astype(BF16)
        proj = _in_proj(xb, w_in_b)
        kb_p, vb_p, kp, vp = _kv_proj(xb, w_in_b, kp, vp, l, depth, 0, n_p)
        kb_s, vb_s, kd, vd = _kv_proj(xb, w_in_b, kd, vd, l, depth, n_p, Bs * Ts)
        o_sb = _sb_prompt(proj, kb_p, vb_p, B, S)
        o_sb = _sb_sample(proj, kb_s, vb_s, cache_sb_k, cache_sb_v, o_sb, l, n_p, Bs, Ts)
        o_hg, s_p = _hgrn(proj, hg_lb_logits, hg_norm[l], None, None, l, 0, B, S)
        o_hg, s_s = _hgrn(proj, hg_lb_logits, hg_norm[l], state_hgrn[l], o_hg, l, n_p, Bs, Ts)
        x, xb = _merge_out(proj, o_sb, o_hg, x, w_out[l].astype(BF16), ln1_g[l], ln1_b[l], alpha)
        last = l == depth - 1
        ranges = [(0, n_p), (n_p, Bs * Ts)] if last else [(0, n_p + Bs * Ts)]
        outs = _moe(xb, x, w_router[l], router_bias[l], w_exp_gate, w_exp_up, w_exp_down,
                    w_sh_gate, w_sh_up, w_sh_down, ln2_g[l], ln2_b[l], alpha, l, ranges)
        if not last:
            x, xb = outs[0]
        sp.append(s_p)
        sd.append(s_s)
    kv_p = (depth, B, S, N_HEADS, HEAD_DIM)
    kv_s = (depth, Bs, Ts, N_HEADS, HEAD_DIM)
    return (outs[0][0].reshape(B, S, D), outs[1][0].reshape(Bs, Ts, D),
            kp.reshape(kv_p), vp.reshape(kv_p), jnp.stack(sp),
            kd.reshape(kv_s), vd.reshape(kv_s), jnp.stack(sd))
```

```python
import functools
import math

import jax
import jax.numpy as jnp
from jax import lax
from jax.experimental import pallas as pl
from jax.experimental.pallas import tpu as pltpu

F32 = jnp.float32
BF16 = jnp.bfloat16

N_HEADS = 8
HEAD_DIM = 128
N_EXPERTS = 64
TOP_K = 8
N_GROUPS = 8
TOPK_GROUPS = 4
GROUP_SIZE = N_EXPERTS // N_GROUPS
ROUTE_SCALE = 2.5
NEG_BIG = -1e9
LB_FLOOR = 1e-30
LN_EPS = 1e-5
RMS_EPS = 1e-6
LOG2E = 1.4426950408889634
N_SPLITS = 9
P_Q, P_HQ, P_HF, P_HI, P_HG, P_GA, P_GB = range(7)

LANES = 128
SUBLANES = 8
VMEM_LIMIT = 48 * 1024 * 1024

HG_CHUNK = 64
HG_SUB = 32
SB_BLOCK = 256
SB_GROUP = 8
WC_LANES = 128
MOE_TILE = 256
MOE_CAP = 64
MOE_CAP_FAST = 48
MOE_EGROUP = 16
MOE_TILES_PER_STEP = 24


def _pick_tile(n, pref):
    if n <= pref:
        return n
    for t in range(pref, 7, -1):
        if n % t == 0 and t % SUBLANES == 0:
            return t
    raise ValueError(f"no tile for {n}")


def _params(*sem):
    return pltpu.CompilerParams(dimension_semantics=sem, vmem_limit_bytes=VMEM_LIMIT)


def _layer_norm(x, g, b):
    mu = jnp.mean(x, axis=-1, keepdims=True)
    xc = x - mu
    var = jnp.mean(xc * xc, axis=-1, keepdims=True)
    return xc * lax.rsqrt(var + LN_EPS) * g + b


def _sigmoid(x):
    return 1.0 / (1.0 + jnp.exp(-x))


def _log_sigmoid(x):
    return jnp.minimum(x, 0.0) - jnp.log(1.0 + jnp.exp(-jnp.abs(x)))


def _ln_kernel(x_ref, g_ref, b_ref, y_ref, yb_ref):
    y = _layer_norm(x_ref[...], g_ref[...], b_ref[...])
    y_ref[...] = y
    yb_ref[...] = y.astype(BF16)


def _entry_ln(x, g, b):
    T, D = x.shape
    tm = _pick_tile(T, 512)
    return pl.pallas_call(
        _ln_kernel,
        grid=(T // tm,),
        in_specs=[pl.BlockSpec((tm, D), lambda i: (i, 0)),
                  pl.BlockSpec((1, D), lambda i: (0, 0)),
                  pl.BlockSpec((1, D), lambda i: (0, 0))],
        out_specs=[pl.BlockSpec((tm, D), lambda i: (i, 0)),
                   pl.BlockSpec((tm, D), lambda i: (i, 0))],
        out_shape=[jax.ShapeDtypeStruct((T, D), F32), jax.ShapeDtypeStruct((T, D), BF16)],
        compiler_params=_params("parallel"),
        name="entry_ln",
    )(x, g.reshape(1, D), b.reshape(1, D))


def _in_proj_kernel(x_ref, w_ref, o_ref):
    o_ref[...] = jnp.dot(x_ref[...], w_ref[...], preferred_element_type=F32)


def _in_proj(xb, w_in_b):
    T, D = xb.shape
    tm = _pick_tile(T, 1024)
    return pl.pallas_call(
        _in_proj_kernel,
        grid=(N_SPLITS - 2, T // tm),
        in_specs=[pl.BlockSpec((tm, D), lambda j, i: (i, 0)),
                  pl.BlockSpec((D, D), lambda j, i: (0, jnp.where(j >= 1, j + 2, j)))],
        out_specs=pl.BlockSpec((None, tm, D), lambda j, i: (j, i, 0)),
        out_shape=jax.ShapeDtypeStruct((N_SPLITS - 2, T, D), F32),
        compiler_params=_params("parallel", "parallel"),
        name="in_proj",
    )(xb, w_in_b)


def _kv_proj_kernel(x_ref, wk_ref, wv_ref, *refs):
    kb_ref, vb_ref, ko_ref, vo_ref = refs[-4:]
    tm = x_ref.shape[0]
    x = x_ref[...]
    for w_ref, b_ref, o_ref in ((wk_ref, kb_ref, ko_ref), (wv_ref, vb_ref, vo_ref)):
        y = jnp.dot(x, w_ref[...], preferred_element_type=F32)
        b_ref[...] = y.astype(BF16)
        for h in range(N_HEADS):
            o_ref[pl.ds(h, tm, stride=N_HEADS), :] = y[:, h * HEAD_DIM:(h + 1) * HEAD_DIM]


def _kv_proj(xb, w_in_b, k_prev, v_prev, layer, depth, row0, n):
    D = xb.shape[1]
    tm = _pick_tile(n, 512)
    assert row0 % tm == 0
    rb = row0 // tm
    out_spec = pl.BlockSpec((None, tm * N_HEADS, HEAD_DIM), lambda i: (layer, i, 0))
    out_sds = jax.ShapeDtypeStruct((depth, n * N_HEADS, HEAD_DIM), F32)
    args = [xb, w_in_b, w_in_b]
    in_specs = [pl.BlockSpec((tm, D), lambda i: (rb + i, 0)),
                pl.BlockSpec((D, D), lambda i: (0, 1)),
                pl.BlockSpec((D, D), lambda i: (0, 2))]
    aliases = {}
    if k_prev is not None:
        args += [k_prev, v_prev]
        in_specs += [pl.BlockSpec(memory_space=pl.ANY)] * 2
        aliases = {3: 2, 4: 3}
    return pl.pallas_call(
        _kv_proj_kernel,
        grid=(n // tm,),
        in_specs=in_specs,
        out_specs=[pl.BlockSpec((tm, D), lambda i: (i, 0)), pl.BlockSpec((tm, D), lambda i: (i, 0)),
                   out_spec, out_spec],
        out_shape=[jax.ShapeDtypeStruct((n, D), BF16), jax.ShapeDtypeStruct((n, D), BF16),
                   out_sds, out_sds],
        input_output_aliases=aliases,
        compiler_params=_params("parallel"),
        name="kv_proj",
    )(*args)


def _neg_tri(tk):
    r = lax.broadcasted_iota(jnp.int32, (tk, tk), 0)
    c = lax.broadcasted_iota(jnp.int32, (tk, tk), 1)
    return jnp.where(r >= c, -1.0, 0.0).astype(BF16)


def _strict_lower(tq, tk):
    return (lax.broadcasted_iota(jnp.int32, (tq, tk), 1)
            < lax.broadcasted_iota(jnp.int32, (tq, tk), 0))


def _sb_z(qs, kb):
    return lax.dot_general(qs, kb, (((1,), (1,)), ((), ())), preferred_element_type=F32)


def _sb_cumsum_lhs(z, tk, mask):
    sp = jnp.maximum(z, 0.0) + jnp.log(1.0 + jnp.exp2(jnp.abs(z) * (-LOG2E)))
    if mask is not None:
        sp = jnp.where(mask, sp, 0.0)
    sp = sp.astype(BF16)
    n = z.shape[1] // tk
    return sp if n == 1 else jnp.concatenate([sp[:, a * tk:(a + 1) * tk] for a in range(n)], axis=0)


def _sb_weights(z, c, carry, tk, mask):
    tq = z.shape[0]
    n = z.shape[1] // tk
    ws = [None] * n
    for a in range(n - 1, -1, -1):
        ca = c[a * tq:(a + 1) * tq, :]
        w = jnp.exp2((z[:, a * tk:(a + 1) * tk] + ca + carry) * LOG2E)
        if mask is not None:
            w = jnp.where(mask, w, 0.0)
        ws[a] = w.astype(BF16)
        carry = carry + ca[:, 0:1]
    return (ws[0] if n == 1 else jnp.concatenate(ws, axis=1)), carry


def _sb_prompt_kernel(q_ref, kb_scr, vb_scr, o_ref, tri_scr, *, blk):
    i = pl.program_id(2)

    @pl.when(i == 0)
    def _():
        tri_scr[...] = _neg_tri(blk)

    qs = (q_ref[...] * (1.0 / math.sqrt(HEAD_DIM))).astype(BF16)

    def block_rows(j):
        return pl.ds(pl.multiple_of(j * blk, blk), blk)

    def sweep(j_lo, n, co, diag_last, zs=None):
        carry, o = co
        rows = [block_rows(j_lo + a) for a in range(n)]
        masks = [_strict_lower(blk, blk) if diag_last and a == n - 1 else None for a in range(n)]
        if zs is None:
            zs = [_sb_z(qs, kb_scr[r, :]) for r in rows]
        cs = [jnp.dot(_sb_cumsum_lhs(z, blk, m), tri_scr[...], preferred_element_type=F32)
              for z, m in zip(zs, masks)]
        ws = []
        for a in range(n - 1, -1, -1):
            w, carry = _sb_weights(zs[a], cs[a], carry, blk, masks[a])
            ws.append(w)
        if n == 1:
            return carry, o + jnp.dot(ws[0], vb_scr[rows[0], :], preferred_element_type=F32)
        v_desc = jnp.concatenate([vb_scr[rows[a], :] for a in range(n - 1, -1, -1)], axis=0)
        return carry, o + jnp.dot(jnp.concatenate(ws, axis=1), v_desc, preferred_element_type=F32)

    co = (jnp.zeros((blk, 1), F32), jnp.zeros((blk, HEAD_DIM), F32))
    rem = i % SB_GROUP
    co = lax.switch(rem, [functools.partial(lambda r, co: sweep(i - r, r + 1, co, True), r)
                          for r in range(SB_GROUP)], co)
    co = lax.fori_loop(0, i // SB_GROUP,
                       lambda p, co: sweep(i - rem - SB_GROUP * (p + 1), SB_GROUP, co, False), co)
    o_ref[...] = co[1]


def _sb_prompt(proj, kb, vb, B, S):
    D = N_HEADS * HEAD_DIM
    t_all = proj.shape[1]
    blk = _pick_tile(S, SB_BLOCK)
    nq = S // blk
    return pl.pallas_call(
        functools.partial(_sb_prompt_kernel, blk=blk),
        grid=(B, N_HEADS, nq),
        in_specs=[pl.BlockSpec((None, blk, HEAD_DIM), lambda b, h, i: (P_Q, b * nq + i, h)),
                  pl.BlockSpec((S, HEAD_DIM), lambda b, h, i: (b, h)),
                  pl.BlockSpec((S, HEAD_DIM), lambda b, h, i: (b, h))],
        out_specs=pl.BlockSpec((blk, HEAD_DIM), lambda b, h, i: (b * nq + i, h)),
        out_shape=jax.ShapeDtypeStruct((t_all, D), F32),
        scratch_shapes=[pltpu.VMEM((blk, blk), BF16)],
        compiler_params=_params("parallel", "parallel", "arbitrary"),
        name="sb_prompt",
    )(proj, kb, vb)


def _sb_sample_kernel(q_ref, k_ref, v_ref, pk_ref, pv_ref, o_in_ref, o_ref, *, pblk):
    del o_in_ref
    T = q_ref.shape[0]
    P = pk_ref.shape[0] // N_HEADS

    def head_rows(ref, h):
        return ref[pl.ds(h, P, stride=N_HEADS), :].astype(BF16)

    heads = range(N_HEADS)
    cols = [slice(h * HEAD_DIM, (h + 1) * HEAD_DIM) for h in heads]
    mask = _strict_lower(T, T)
    qs = [(q_ref[:, cols[h]] * (1.0 / math.sqrt(HEAD_DIM))).astype(BF16) for h in heads]
    z_new = [_sb_z(qs[h], k_ref[:, cols[h]]) for h in heads]
    z_past = [_sb_z(qs[h], head_rows(pk_ref, h)) for h in heads]
    lhs_new = jnp.concatenate([_sb_cumsum_lhs(z_new[h], T, mask) for h in heads], axis=0)
    lhs_past = jnp.concatenate([_sb_cumsum_lhs(z_past[h], pblk, None) for h in heads], axis=0)
    c_new = jnp.dot(lhs_new, _neg_tri(T), preferred_element_type=F32)
    c_past = jnp.dot(lhs_past, _neg_tri(pblk), preferred_element_type=F32)
    rows_past = (P // pblk) * T
    for h in heads:
        w_new, carry = _sb_weights(z_new[h], c_new[h * T:(h + 1) * T], jnp.zeros((T, 1), F32), T, mask)
        w_past, _ = _sb_weights(z_past[h], c_past[h * rows_past:(h + 1) * rows_past], carry, pblk, None)
        o_ref[:, cols[h]] = (jnp.dot(w_new, v_ref[:, cols[h]], preferred_element_type=F32)
                             + jnp.dot(w_past, head_rows(pv_ref, h), preferred_element_type=F32))


def _sb_sample(proj, kb, vb, cache_k, cache_v, o_sb, layer, row0, B, T):
    D = N_HEADS * HEAD_DIM
    depth, _, P = cache_k.shape[:3]
    pblk = _pick_tile(P, SB_BLOCK)
    rb = row0 // T
    cache_k = cache_k.reshape(depth, B, P * N_HEADS, HEAD_DIM)
    cache_v = cache_v.reshape(depth, B, P * N_HEADS, HEAD_DIM)
    cache_spec = pl.BlockSpec((None, None, P * N_HEADS, HEAD_DIM), lambda b: (layer, b, 0, 0))
    return pl.pallas_call(
        functools.partial(_sb_sample_kernel, pblk=pblk),
        grid=(B,),
        in_specs=[pl.BlockSpec((None, T, D), lambda b: (P_Q, rb + b, 0)),
                  pl.BlockSpec((T, D), lambda b: (b, 0)),
                  pl.BlockSpec((T, D), lambda b: (b, 0)),
                  cache_spec, cache_spec,
                  pl.BlockSpec(memory_space=pl.ANY)],
        out_specs=pl.BlockSpec((T, D), lambda b: (rb + b, 0)),
        out_shape=jax.ShapeDtypeStruct(o_sb.shape, F32),
        input_output_aliases={5: 0},
        compiler_params=_params("parallel"),
        name="sb_sample",
    )(proj, kb, vb, cache_k, cache_v, o_sb)


def _cumsum_rows(x):
    n = x.shape[0]
    row = lax.broadcasted_iota(jnp.int32, x.shape, 0)
    s = 1
    while s < n:
        x = x + jnp.where(row >= s, pltpu.roll(x, s, 0), 0.0)
        s *= 2
    return x


def _hgrn_kernel(*refs, layer, has_s0):
    if has_s0:
        q_ref, f_ref, i_ref, g_ref, lbl_ref, norm_ref, s0_ref, _, o_ref, sf_ref, st_scr = refs
    else:
        q_ref, f_ref, i_ref, g_ref, lbl_ref, norm_ref, o_ref, sf_ref, st_scr = refs
    c = pl.program_id(1)
    C = q_ref.shape[0]
    n_sub = C // HG_SUB

    @pl.when(c == 0)
    def _():
        for h in range(N_HEADS):
            if has_s0:
                st_scr[h] = s0_ref[h].T
            else:
                st_scr[h] = jnp.zeros((HEAD_DIM, HEAD_DIM), F32)

    lg = lbl_ref[...]
    ex = jnp.exp(lg - jnp.max(lg, axis=0, keepdims=True))
    soft = ex / jnp.sum(ex, axis=0, keepdims=True)
    lb = jnp.zeros((1, lg.shape[1]), F32)
    for m in range(1, layer + 1):
        lb = lb + soft[m:m + 1, :]

    a = f_ref[...]
    x1 = _log_sigmoid(a)
    x2 = _log_sigmoid(-a) + jnp.log(jnp.maximum(lb, LB_FLOOR))
    logf = jnp.maximum(x1, x2) + jnp.log(1.0 + jnp.exp(-jnp.abs(x1 - x2)))
    logf = jnp.minimum(logf, 0.0)
    kk_all = 1.0 - jnp.exp(logf)
    b_all = _cumsum_rows(logf)
    qraw = q_ref[...]
    q_all = qraw * _sigmoid(qraw)
    graw = g_ref[...]
    gate_all = graw * _sigmoid(graw)
    v_all = i_ref[...]

    row = lax.broadcasted_iota(jnp.int32, (C, HEAD_DIM), 0)
    for h in range(N_HEADS):
        hs = slice(h * HEAD_DIM, (h + 1) * HEAD_DIM)
        b = b_all[:, hs]
        q = q_all[:, hs]
        kk = kk_all[:, hs]
        v = v_all[:, hs]
        vb = v.astype(BF16)
        st = st_scr[h]
        qt = (q * jnp.exp(b)).astype(BF16)
        o = lax.dot_general(qt, st.astype(BF16), (((1,), (1,)), ((), ())),
                            preferred_element_type=F32)
        parts = []
        for i in range(n_sub):
            lo_r, hi_r = i * HG_SUB, (i + 1) * HG_SUB
            if i == 0:
                ref_row = jnp.zeros((1, HEAD_DIM), F32)
            else:
                ref_row = b[lo_r - 1:lo_r, :]
            qh = (q[lo_r:hi_r] * jnp.exp(b[lo_r:hi_r] - ref_row)).astype(BF16)
            kh = (kk[0:hi_r] * jnp.exp(ref_row - b[0:hi_r])).astype(BF16)
            sc = lax.dot_general(qh, kh, (((1,), (1,)), ((), ())),
                                 preferred_element_type=F32)
            tpos = lax.broadcasted_iota(jnp.int32, (HG_SUB, hi_r), 0) + lo_r
            spos = lax.broadcasted_iota(jnp.int32, (HG_SUB, hi_r), 1)
            sc = jnp.where(spos <= tpos, sc, 0.0)
            parts.append(jnp.dot(sc.astype(BF16), vb[0:hi_r], preferred_element_type=F32))
        o = o + jnp.concatenate(parts, axis=0)
        b_last = b[C - 1:C, :]
        ke = (kk * jnp.exp(b_last - b)).astype(BF16)
        upd = lax.dot_general(vb, ke, (((0,), (0,)), ((), ())), preferred_element_type=F32)
        st_scr[h] = st * jnp.exp(b_last) + upd
        ms = jnp.mean(o * o, axis=-1, keepdims=True)
        o_ref[:, hs] = o * lax.rsqrt(ms + RMS_EPS) * norm_ref[...] * gate_all[:, hs]

    @pl.when(c == pl.num_programs(1) - 1)
    def _():
        for h in range(N_HEADS):
            sf_ref[h] = st_scr[h].T


def _hgrn(proj, lb_logits, hg_norm, s0, o_prev, layer, row0, B, T):
    D = N_HEADS * HEAD_DIM
    C = HG_CHUNK
    assert T % C == 0 and row0 % C == 0
    nc = T // C
    rb = row0 // C
    depth = lb_logits.shape[0]
    t_all = proj.shape[1]

    def act_spec(split):
        return pl.BlockSpec((None, C, D), lambda b, c: (split, rb + b * nc + c, 0))

    in_specs = [act_spec(P_HQ), act_spec(P_HF), act_spec(P_HI), act_spec(P_HG),
                pl.BlockSpec((depth, D), lambda b, c: (0, 0)),
                pl.BlockSpec((1, HEAD_DIM), lambda b, c: (0, 0))]
    args = [proj, proj, proj, proj, lb_logits, hg_norm.reshape(1, HEAD_DIM)]
    if s0 is not None:
        in_specs.append(pl.BlockSpec((None, N_HEADS, HEAD_DIM, HEAD_DIM), lambda b, c: (b, 0, 0, 0)))
        in_specs.append(pl.BlockSpec(memory_space=pl.ANY))
        args += [s0, o_prev]
    return pl.pallas_call(
        functools.partial(_hgrn_kernel, layer=layer, has_s0=s0 is not None),
        grid=(B, nc),
        in_specs=in_specs,
        out_specs=[pl.BlockSpec((C, D), lambda b, c: (rb + b * nc + c, 0)),
                   pl.BlockSpec((None, N_HEADS, HEAD_DIM, HEAD_DIM), lambda b, c: (b, 0, 0, 0))],
        out_shape=[jax.ShapeDtypeStruct((t_all, D), F32),
                   jax.ShapeDtypeStruct((B, N_HEADS, HEAD_DIM, HEAD_DIM), F32)],
        input_output_aliases={} if s0 is None else {len(args) - 1: 0},
        scratch_shapes=[pltpu.VMEM((N_HEADS, HEAD_DIM, HEAD_DIM), F32)],
        compiler_params=_params("parallel", "arbitrary"),
        name="hgrn2",
    )(*args)


def _merge_out_kernel(ga_ref, gb_ref, oa_ref, ob_ref, x_ref, w_ref, g_ref, b_ref, y_ref, yb_ref, *, alpha):
    merged = _sigmoid(ga_ref[...]) * oa_ref[...] + _sigmoid(gb_ref[...]) * ob_ref[...]
    r = alpha * x_ref[...] + jnp.dot(merged.astype(BF16), w_ref[...], preferred_element_type=F32)
    y = _layer_norm(r, g_ref[...], b_ref[...])
    y_ref[...] = y
    yb_ref[...] = y.astype(BF16)


def _merge_out(proj, o_sb, o_hg, x, w_out_b, g, b, alpha):
    T, D = x.shape
    tm = _pick_tile(T, 256)
    row = pl.BlockSpec((tm, D), lambda i: (i, 0))
    vec = pl.BlockSpec((1, D), lambda i: (0, 0))
    return pl.pallas_call(
        functools.partial(_merge_out_kernel, alpha=alpha),
        grid=(T // tm,),
        in_specs=[pl.BlockSpec((None, tm, D), lambda i: (P_GA, i, 0)),
                  pl.BlockSpec((None, tm, D), lambda i: (P_GB, i, 0)),
                  row, row, row,
                  pl.BlockSpec((D, D), lambda i: (0, 0)), vec, vec],
        out_specs=[row, row],
        out_shape=[jax.ShapeDtypeStruct((T, D), F32), jax.ShapeDtypeStruct((T, D), BF16)],
        compiler_params=_params("parallel"),
        name="merge_out",
    )(proj, proj, o_sb, o_hg, x, w_out_b, g.reshape(1, D), b.reshape(1, D))


def _router_kernel(x_ref, wr_ref, bias_ref, wc_ref, wo_ref, pos_ref, cnt_ref):
    tm = x_ref.shape[0]
    logits = lax.dot_general(wr_ref[...], x_ref[...], (((1,), (1,)), ((), ())),
                             precision=lax.Precision.HIGHEST, preferred_element_type=F32)
    s = _sigmoid(logits)
    sb = s + bias_ref[...]
    sub = lax.broadcasted_iota(jnp.int32, (GROUP_SIZE, tm), 0)
    gs = []
    for g in range(N_GROUPS):
        blk = sb[g * GROUP_SIZE:(g + 1) * GROUP_SIZE, :]
        m1 = jnp.max(blk, axis=0, keepdims=True)
        first = jnp.min(jnp.where(blk == m1, sub, GROUP_SIZE), axis=0, keepdims=True)
        m2 = jnp.max(jnp.where(sub == first, -jnp.inf, blk), axis=0, keepdims=True)
        gs.append(m1 + m2)
    masked = []
    for g in range(N_GROUPS):
        beat = jnp.zeros((1, tm), jnp.int32)
        for g2 in range(N_GROUPS):
            if g2 == g:
                continue
            if g2 < g:
                beat = beat + (gs[g2] >= gs[g]).astype(jnp.int32)
            else:
                beat = beat + (gs[g2] > gs[g]).astype(jnp.int32)
        keep = beat < TOPK_GROUPS
        masked.append(jnp.where(keep, sb[g * GROUP_SIZE:(g + 1) * GROUP_SIZE, :], NEG_BIG))
    m = jnp.concatenate(masked, axis=0)
    erow = lax.broadcasted_iota(jnp.int32, (N_EXPERTS, tm), 0)
    sel = jnp.zeros((N_EXPERTS, tm), jnp.bool_)
    for _ in range(TOP_K):
        mx = jnp.max(m, axis=0, keepdims=True)
        first = jnp.min(jnp.where(m == mx, erow, N_EXPERTS), axis=0, keepdims=True)
        hit = erow == first
        sel = jnp.logical_or(sel, hit)
        m = jnp.where(hit, -jnp.inf, m)
    w = jnp.where(sel, s, 0.0)
    w = w / jnp.sum(w, axis=0, keepdims=True) * ROUTE_SCALE
    sel_f = jnp.where(sel, 1.0, 0.0)
    earlier = jnp.where(lax.broadcasted_iota(jnp.int32, (tm, tm), 0)
                        < lax.broadcasted_iota(jnp.int32, (tm, tm), 1), 1.0, 0.0).astype(BF16)
    slot = jnp.dot(sel_f.astype(BF16), earlier, preferred_element_type=F32)
    fits = jnp.logical_and(sel, slot < MOE_CAP)
    zpad = jnp.zeros((WC_LANES - N_EXPERTS, tm), F32)
    wc_ref[...] = jnp.concatenate([jnp.where(fits, w, 0.0), zpad], axis=0).T
    wo_ref[...] = jnp.concatenate([jnp.where(fits, 0.0, w), zpad], axis=0).T
    pos_ref[...] = jnp.where(fits, slot, -1.0)
    cnt_ref[...] = jnp.broadcast_to(jnp.sum(sel_f, axis=1, keepdims=True), (N_EXPERTS, WC_LANES))


def _router(x, w_router, bias):
    T, D = x.shape
    tm = MOE_TILE
    assert T % tm == 0
    nt = T // tm
    return pl.pallas_call(
        _router_kernel,
        grid=(nt,),
        in_specs=[pl.BlockSpec((tm, D), lambda i: (i, 0)),
                  pl.BlockSpec((N_EXPERTS, D), lambda i: (0, 0)),
                  pl.BlockSpec((N_EXPERTS, 1), lambda i: (0, 0))],
        out_specs=[pl.BlockSpec((tm, WC_LANES), lambda i: (i, 0)),
                   pl.BlockSpec((tm, WC_LANES), lambda i: (i, 0)),
                   pl.BlockSpec((None, N_EXPERTS, tm), lambda i: (i, 0, 0)),
                   pl.BlockSpec((None, N_EXPERTS, WC_LANES), lambda i: (i, 0, 0))],
        out_shape=[jax.ShapeDtypeStruct((T, WC_LANES), F32),
                   jax.ShapeDtypeStruct((T, WC_LANES), F32),
                   jax.ShapeDtypeStruct((nt, N_EXPERTS, tm), F32),
                   jax.ShapeDtypeStruct((nt, N_EXPERTS, WC_LANES), F32)],
        compiler_params=_params("parallel"),
        name="router",
    )(x, w_router.T, bias.reshape(N_EXPERTS, 1))


def _slot_one_hot(pos_ref, e0, n, ncap, tm):
    slot = lax.broadcasted_iota(jnp.int32, (ncap, tm), 0).astype(F32)
    return jnp.concatenate([jnp.where(pos_ref[e:e + 1, :] == slot, 1.0, 0.0).astype(BF16)
                            for e in range(e0, e0 + n)], axis=0)


def _when_slots_used(used, body):
    @pl.when(used <= MOE_CAP_FAST)
    def _():
        body(MOE_CAP_FAST)

    @pl.when(used > MOE_CAP_FAST)
    def _():
        body(MOE_CAP)


def _dispatch_kernel(used_ref, xb_ref, wc_ref, pos_ref, xs_ref):
    tm = xb_ref.shape[0]
    wc = wc_ref[...]
    hi = wc.astype(BF16)
    lo = (wc - hi.astype(F32)).astype(BF16)
    x_aug = jnp.concatenate([xb_ref[...], hi[:, :N_EXPERTS], lo[:, :N_EXPERTS]], axis=1)

    def run(ncap):
        for g in range(N_EXPERTS // MOE_EGROUP):
            p = _slot_one_hot(pos_ref, g * MOE_EGROUP, MOE_EGROUP, ncap, tm)
            rows = jnp.dot(p, x_aug, preferred_element_type=F32).astype(BF16)
            for k in range(MOE_EGROUP):
                r0 = (g * MOE_EGROUP + k) * MOE_CAP
                xs_ref[r0:r0 + ncap, :] = rows[k * ncap:(k + 1) * ncap, :]
                if ncap < MOE_CAP:
                    xs_ref[r0 + ncap:r0 + MOE_CAP, :] = jnp.zeros((MOE_CAP - ncap, x_aug.shape[1]), BF16)

    _when_slots_used(used_ref[pl.program_id(0)], run)


def _dispatch(xb, wc, pos, used):
    T, D = xb.shape
    nt = T // MOE_TILE
    return pl.pallas_call(
        _dispatch_kernel,
        grid_spec=pltpu.PrefetchScalarGridSpec(
            num_scalar_prefetch=1,
            grid=(nt,),
            in_specs=[pl.BlockSpec((MOE_TILE, D), lambda i, u: (i, 0)),
                      pl.BlockSpec((MOE_TILE, WC_LANES), lambda i, u: (i, 0)),
                      pl.BlockSpec((None, N_EXPERTS, MOE_TILE), lambda i, u: (i, 0, 0))],
            out_specs=pl.BlockSpec((None, N_EXPERTS * MOE_CAP, D + WC_LANES), lambda i, u: (i, 0, 0))),
        out_shape=jax.ShapeDtypeStruct((nt, N_EXPERTS * MOE_CAP, D + WC_LANES), BF16),
        compiler_params=_params("parallel"),
        name="moe_dispatch",
    )(used, xb, wc, pos)


def _swiglu_rows(x, col, wg_ref, wu_ref, wd_ref):
    hg = jnp.dot(x, wg_ref[...].astype(BF16), preferred_element_type=F32)
    hu = jnp.dot(x, wu_ref[...].astype(BF16), preferred_element_type=F32)
    h = hg * _sigmoid(hg) * hu
    if col is not None:
        h = h * col
    return jnp.dot(h.astype(BF16), wd_ref[...].astype(BF16), preferred_element_type=F32)


def _experts_kernel(used_ref, xs_ref, wg32_ref, wu32_ref, wd32_ref, ys_ref, wg_ref, wu_ref, wd_ref):
    e = pl.program_id(0)
    j = pl.program_id(1)

    @pl.when(j == 0)
    def _():
        wg_ref[...] = wg32_ref[...].astype(BF16)
        wu_ref[...] = wu32_ref[...].astype(BF16)
        wd_ref[...] = wd32_ref[...].astype(BF16)

    tb, cap, width = xs_ref.shape
    d = width - WC_LANES

    def run(ncap):
        xa = xs_ref[:, 0:ncap, :].reshape(tb * ncap, width)
        aug = xa[:, d:].astype(F32)
        lane = lax.broadcasted_iota(jnp.int32, aug.shape, 1)
        mine = jnp.logical_or(lane == e, lane == e + N_EXPERTS)
        col = jnp.sum(jnp.where(mine, aug, 0.0), axis=1, keepdims=True)
        y = _swiglu_rows(xa[:, :d], col, wg_ref, wu_ref, wd_ref)
        ys_ref[:, 0:ncap, :] = y.astype(BF16).reshape(tb, ncap, d)
        if ncap < cap:
            ys_ref[:, ncap:cap, :] = jnp.zeros((tb, cap - ncap, d), BF16)

    _when_slots_used(used_ref[e * pl.num_programs(1) + j], run)


def _experts(xs, cnt, wg, wu, wd, layer):
    nt, _, width = xs.shape
    _, n_e, D, F = wg.shape
    tb = _pick_tile(nt, MOE_TILES_PER_STEP) if nt > MOE_TILES_PER_STEP else nt
    if nt % tb:
        tb = 1
    steps = nt // tb
    used = jnp.max(cnt.reshape(steps, tb, n_e), axis=1).T.reshape(n_e * steps).astype(jnp.int32)
    return pl.pallas_call(
        _experts_kernel,
        grid_spec=pltpu.PrefetchScalarGridSpec(
            num_scalar_prefetch=1,
            grid=(n_e, steps),
            in_specs=[pl.BlockSpec((tb, MOE_CAP, width), lambda e, j, u: (j, e, 0)),
                      pl.BlockSpec((None, None, D, F), lambda e, j, u: (layer, e, 0, 0)),
                      pl.BlockSpec((None, None, D, F), lambda e, j, u: (layer, e, 0, 0)),
                      pl.BlockSpec((None, None, F, D), lambda e, j, u: (layer, e, 0, 0))],
            out_specs=pl.BlockSpec((tb, MOE_CAP, D), lambda e, j, u: (j, e, 0)),
            scratch_shapes=[pltpu.VMEM((D, F), BF16), pltpu.VMEM((D, F), BF16),
                            pltpu.VMEM((F, D), BF16)]),
        out_shape=jax.ShapeDtypeStruct((nt, n_e * MOE_CAP, D), BF16),
        compiler_params=_params("parallel", "arbitrary"),
        name="moe_experts",
    )(used, xs, wg, wu, wd)


def _combine_kernel(used_ref, ys_ref, pos_ref, x_ref, xb_ref, yo_ref, sg_ref, su_ref, sd_ref, g_ref,
                    b_ref, y_ref, yb_ref, *, alpha, t0):
    tm = x_ref.shape[0]

    def run(ncap):
        acc = yo_ref[...] + _swiglu_rows(xb_ref[...], None, sg_ref, su_ref, sd_ref)
        for g in range(N_EXPERTS // MOE_EGROUP):
            p = _slot_one_hot(pos_ref, g * MOE_EGROUP, MOE_EGROUP, ncap, tm)
            ys = jnp.concatenate([ys_ref[e * MOE_CAP:e * MOE_CAP + ncap, :]
                                  for e in range(g * MOE_EGROUP, (g + 1) * MOE_EGROUP)], axis=0)
            acc = acc + lax.dot_general(p, ys, (((0,), (0,)), ((), ())), preferred_element_type=F32)
        y = _layer_norm(alpha * x_ref[...] + acc, g_ref[...], b_ref[...])
        y_ref[...] = y
        yb_ref[...] = y.astype(BF16)

    _when_slots_used(used_ref[t0 + pl.program_id(0)], run)


def _combine(ys, pos, used, x, xb, y_over, sg, su, sd, g, b, alpha, layer, row0, n):
    D = x.shape[1]
    F = sg.shape[2]
    assert row0 % MOE_TILE == 0 and n % MOE_TILE == 0
    t0 = row0 // MOE_TILE
    row_in = pl.BlockSpec((MOE_TILE, D), lambda i, u: (t0 + i, 0))
    row_out = pl.BlockSpec((MOE_TILE, D), lambda i, u: (i, 0))
    vec = pl.BlockSpec((1, D), lambda i, u: (0, 0))
    return pl.pallas_call(
        functools.partial(_combine_kernel, alpha=alpha, t0=t0),
        grid_spec=pltpu.PrefetchScalarGridSpec(
            num_scalar_prefetch=1,
            grid=(n // MOE_TILE,),
            in_specs=[pl.BlockSpec((None, N_EXPERTS * MOE_CAP, D), lambda i, u: (t0 + i, 0, 0)),
                      pl.BlockSpec((None, N_EXPERTS, MOE_TILE), lambda i, u: (t0 + i, 0, 0)),
                      row_in, row_in, row_in,
                      pl.BlockSpec((None, D, F), lambda i, u: (layer, 0, 0)),
                      pl.BlockSpec((None, D, F), lambda i, u: (layer, 0, 0)),
                      pl.BlockSpec((None, F, D), lambda i, u: (layer, 0, 0)),
                      vec, vec],
            out_specs=[row_out, row_out]),
        out_shape=[jax.ShapeDtypeStruct((n, D), F32), jax.ShapeDtypeStruct((n, D), BF16)],
        compiler_params=_params("parallel"),
        name="moe_combine",
    )(used, ys, pos, x, xb, y_over, sg, su, sd, g.reshape(1, D), b.reshape(1, D))


def _dense_experts_kernel(xb_ref, wo_ref, wg_ref, wu_ref, wd_ref, y_ref):
    e = pl.program_id(1)

    @pl.when(e == 0)
    def _():
        y_ref[...] = jnp.zeros_like(y_ref)

    lane = lax.broadcasted_iota(jnp.int32, wo_ref.shape, 1)
    col = jnp.sum(jnp.where(lane == e, wo_ref[...], 0.0), axis=1, keepdims=True)
    y_ref[...] += _swiglu_rows(xb_ref[...], col, wg_ref, wu_ref, wd_ref)


def _dense_experts(xb, wo, wg, wu, wd, layer):
    T, D = xb.shape
    _, n_e, _, F = wg.shape
    tm = _pick_tile(T, 1024)
    return pl.pallas_call(
        _dense_experts_kernel,
        grid=(T // tm, n_e),
        in_specs=[pl.BlockSpec((tm, D), lambda i, e: (i, 0)),
                  pl.BlockSpec((tm, WC_LANES), lambda i, e: (i, 0)),
                  pl.BlockSpec((None, None, D, F), lambda i, e: (layer, e, 0, 0)),
                  pl.BlockSpec((None, None, D, F), lambda i, e: (layer, e, 0, 0)),
                  pl.BlockSpec((None, None, F, D), lambda i, e: (layer, e, 0, 0))],
        out_specs=pl.BlockSpec((tm, D), lambda i, e: (i, 0)),
        out_shape=jax.ShapeDtypeStruct((T, D), F32),
        compiler_params=_params("parallel", "arbitrary"),
        name="moe_dense_overflow",
    )(xb, wo, wg, wu, wd)


def _moe(xb, x, w_router, bias, wg, wu, wd, sg, su, sd, g, b, alpha, layer, ranges):
    wc, wo, pos, cnt = _router(x, w_router, bias)
    cnt = cnt[:, :, 0]
    tile_used = jnp.max(cnt, axis=1).astype(jnp.int32)
    ys = _experts(_dispatch(xb, wc, pos, tile_used), cnt, wg, wu, wd, layer)
    y_over = lax.cond(jnp.max(cnt) > MOE_CAP,
                      lambda: _dense_experts(xb, wo, wg, wu, wd, layer),
                      lambda: jnp.zeros(x.shape, F32))
    return [_combine(ys, pos, tile_used, x, xb, y_over, sg, su, sd, g, b, alpha, layer, row0, n)
            for row0, n in ranges]


def kernel(x_prompt, x_sample, cache_sb_k, cache_sb_v, state_hgrn, ln_in_g, ln_in_b, w_in, w_out, hg_norm, hg_lb_logits, ln1_g, ln1_b, w_router, router_bias, w_exp_gate, w_exp_up, w_exp_down, w_sh_gate, w_sh_up, w_sh_down, ln2_g, ln2_b):
    B, S, D = x_prompt.shape
    Bs, Ts, _ = x_sample.shape
    depth = w_in.shape[0]
    P = cache_sb_k.shape[2]
    n_p = B * S
    alpha = (2 * depth) ** 0.25

    x_all = jnp.concatenate([x_prompt.reshape(n_p, D), x_sample.reshape(Bs * Ts, D)], axis=0)
    x, xb = _entry_ln(x_all, ln_in_g, ln_in_b)

    sp, sd = [], []
    kp = vp = kd = vd = None
    for l in range(depth):
        w_in_b = w_in[l].astype(BF16)
        proj = _in_proj(xb, w_in_b)
        kb_p, vb_p, kp, vp = _kv_proj(xb, w_in_b, kp, vp, l, depth, 0, n_p)
        kb_s, vb_s, kd, vd = _kv_proj(xb, w_in_b, kd, vd, l, depth, n_p, Bs * Ts)
        o_sb = _sb_prompt(proj, kb_p, vb_p, B, S)
        o_sb = _sb_sample(proj, kb_s, vb_s, cache_sb_k, cache_sb_v, o_sb, l, n_p, Bs, Ts)
        o_hg, s_p = _hgrn(proj, hg_lb_logits, hg_norm[l], None, None, l, 0, B, S)
        o_hg, s_s = _hgrn(proj, hg_lb_logits, hg_norm[l], state_hgrn[l], o_hg, l, n_p, Bs, Ts)
        x, xb = _merge_out(proj, o_sb, o_hg, x, w_out[l].astype(BF16), ln1_g[l], ln1_b[l], alpha)
        last = l == depth - 1
        ranges = [(0, n_p), (n_p, Bs * Ts)] if last else [(0, n_p + Bs * Ts)]
        outs = _moe(xb, x, w_router[l], router_bias[l], w_exp_gate, w_exp_up, w_exp_down,
                    w_sh_gate, w_sh_up, w_sh_down, ln2_g[l], ln2_b[l], alpha, l, ranges)
        if not last:
            x, xb = outs[0]
        sp.append(s_p)
        sd.append(s_s)
    kv_p = (depth, B, S, N_HEADS, HEAD_DIM)
    kv_s = (depth, Bs, Ts, N_HEADS, HEAD_DIM)
    return (outs[0][0].reshape(B, S, D), outs[1][0].reshape(Bs, Ts, D),
            kp.reshape(kv_p), vp.reshape(kv_p), jnp.stack(sp),
            kd.reshape(kv_s), vd.reshape(kv_s), jnp.stack(sd))
```

```python
import functools
import math

import jax
import jax.numpy as jnp
from jax import lax
from jax.experimental import pallas as pl
from jax.experimental.pallas import tpu as pltpu

F32 = jnp.float32
BF16 = jnp.bfloat16

N_HEADS = 8
HEAD_DIM = 128
N_EXPERTS = 64
TOP_K = 8
N_GROUPS = 8
TOPK_GROUPS = 4
GROUP_SIZE = N_EXPERTS // N_GROUPS
ROUTE_SCALE = 2.5
NEG_BIG = -1e9
LB_FLOOR = 1e-30
LN_EPS = 1e-5
RMS_EPS = 1e-6
LOG2E = 1.4426950408889634
N_SPLITS = 9
P_Q, P_HQ, P_HF, P_HI, P_HG, P_GA, P_GB = range(7)

LANES = 128
SUBLANES = 8
VMEM_LIMIT = 48 * 1024 * 1024

HG_CHUNK = 64
HG_SUB = 32
SB_BLOCK = 256
SB_GROUP = 8
WC_LANES = 128
MOE_TILE = 256
MOE_CAP = 64
MOE_CAP_FAST = 48
MOE_EGROUP = 16
MOE_TILES_PER_STEP = 24


def _pick_tile(n, pref):
    if n <= pref:
        return n
    for t in range(pref, 7, -1):
        if n % t == 0 and t % SUBLANES == 0:
            return t
    raise ValueError(f"no tile for {n}")


def _params(*sem):
    return pltpu.CompilerParams(dimension_semantics=sem, vmem_limit_bytes=VMEM_LIMIT)


def _layer_norm(x, g, b):
    mu = jnp.mean(x, axis=-1, keepdims=True)
    xc = x - mu
    var = jnp.mean(xc * xc, axis=-1, keepdims=True)
    return xc * lax.rsqrt(var + LN_EPS) * g + b


def _sigmoid(x):
    return 1.0 / (1.0 + jnp.exp(-x))


def _log_sigmoid(x):
    return jnp.minimum(x, 0.0) - jnp.log(1.0 + jnp.exp(-jnp.abs(x)))


def _ln_kernel(x_ref, g_ref, b_ref, y_ref, yb_ref):
    y = _layer_norm(x_ref[...], g_ref[...], b_ref[...])
    y_ref[...] = y
    yb_ref[...] = y.astype(BF16)


def _entry_ln(x, g, b):
    T, D = x.shape
    tm = _pick_tile(T, 512)
    return pl.pallas_call(
        _ln_kernel,
        grid=(T // tm,),
        in_specs=[pl.BlockSpec((tm, D), lambda i: (i, 0)),
                  pl.BlockSpec((1, D), lambda i: (0, 0)),
                  pl.BlockSpec((1, D), lambda i: (0, 0))],
        out_specs=[pl.BlockSpec((tm, D), lambda i: (i, 0)),
                   pl.BlockSpec((tm, D), lambda i: (i, 0))],
        out_shape=[jax.ShapeDtypeStruct((T, D), F32), jax.ShapeDtypeStruct((T, D), BF16)],
        compiler_params=_params("parallel"),
        name="entry_ln",
    )(x, g.reshape(1, D), b.reshape(1, D))


def _in_proj_kernel(x_ref, w_ref, o_ref):
    o_ref[...] = jnp.dot(x_ref[...], w_ref[...], preferred_element_type=F32)


def _in_proj(xb, w_in_b):
    T, D = xb.shape
    tm = _pick_tile(T, 1024)
    return pl.pallas_call(
        _in_proj_kernel,
        grid=(N_SPLITS - 2, T // tm),
        in_specs=[pl.BlockSpec((tm, D), lambda j, i: (i, 0)),
                  pl.BlockSpec((D, D), lambda j, i: (0, jnp.where(j >= 1, j + 2, j)))],
        out_specs=pl.BlockSpec((None, tm, D), lambda j, i: (j, i, 0)),
        out_shape=jax.ShapeDtypeStruct((N_SPLITS - 2, T, D), F32),
        compiler_params=_params("parallel", "parallel"),
        name="in_proj",
    )(xb, w_in_b)


def _kv_proj_kernel(x_ref, wk_ref, wv_ref, *refs):
    kb_ref, vb_ref, ko_ref, vo_ref = refs[-4:]
    tm = x_ref.shape[0]
    x = x_ref[...]
    for w_ref, b_ref, o_ref in ((wk_ref, kb_ref, ko_ref), (wv_ref, vb_ref, vo_ref)):
        y = jnp.dot(x, w_ref[...], preferred_element_type=F32)
        b_ref[...] = y.astype(BF16)
        for h in range(N_HEADS):
            o_ref[pl.ds(h, tm, stride=N_HEADS), :] = y[:, h * HEAD_DIM:(h + 1) * HEAD_DIM]


def _kv_proj(xb, w_in_b, k_prev, v_prev, layer, depth, row0, n):
    D = xb.shape[1]
    tm = _pick_tile(n, 512)
    assert row0 % tm == 0
    rb = row0 // tm
    out_spec = pl.BlockSpec((None, tm * N_HEADS, HEAD_DIM), lambda i: (layer, i, 0))
    out_sds = jax.ShapeDtypeStruct((depth, n * N_HEADS, HEAD_DIM), F32)
    args = [xb, w_in_b, w_in_b]
    in_specs = [pl.BlockSpec((tm, D), lambda i: (rb + i, 0)),
                pl.BlockSpec((D, D), lambda i: (0, 1)),
                pl.BlockSpec((D, D), lambda i: (0, 2))]
    aliases = {}
    if k_prev is not None:
        args += [k_prev, v_prev]
        in_specs += [pl.BlockSpec(memory_space=pl.ANY)] * 2
        aliases = {3: 2, 4: 3}
    return pl.pallas_call(
        _kv_proj_kernel,
        grid=(n // tm,),
        in_specs=in_specs,
        out_specs=[pl.BlockSpec((tm, D), lambda i: (i, 0)), pl.BlockSpec((tm, D), lambda i: (i, 0)),
                   out_spec, out_spec],
        out_shape=[jax.ShapeDtypeStruct((n, D), BF16), jax.ShapeDtypeStruct((n, D), BF16),
                   out_sds, out_sds],
        input_output_aliases=aliases,
        compiler_params=_params("parallel"),
        name="kv_proj",
    )(*args)


def _neg_tri(tk):
    r = lax.broadcasted_iota(jnp.int32, (tk, tk), 0)
    c = lax.broadcasted_iota(jnp.int32, (tk, tk), 1)
    return jnp.where(r >= c, -1.0, 0.0).astype(BF16)


def _strict_lower(tq, tk):
    return (lax.broadcasted_iota(jnp.int32, (tq, tk), 1)
            < lax.broadcasted_iota(jnp.int32, (tq, tk), 0))


def _sb_z(qs, kb):
    return lax.dot_general(qs, kb, (((1,), (1,)), ((), ())), preferred_element_type=F32)


def _sb_cumsum_lhs(z, tk, mask):
    sp = jnp.maximum(z, 0.0) + jnp.log(1.0 + jnp.exp2(jnp.abs(z) * (-LOG2E)))
    if mask is not None:
        sp = jnp.where(mask, sp, 0.0)
    sp = sp.astype(BF16)
    n = z.shape[1] // tk
    return sp if n == 1 else jnp.concatenate([sp[:, a * tk:(a + 1) * tk] for a in range(n)], axis=0)


def _sb_weights(z, c, carry, tk, mask):
    tq = z.shape[0]
    n = z.shape[1] // tk
    ws = [None] * n
    for a in range(n - 1, -1, -1):
        ca = c[a * tq:(a + 1) * tq, :]
        w = jnp.exp2((z[:, a * tk:(a + 1) * tk] + ca + carry) * LOG2E)
        if mask is not None:
            w = jnp.where(mask, w, 0.0)
        ws[a] = w.astype(BF16)
        carry = carry + ca[:, 0:1]
    return (ws[0] if n == 1 else jnp.concatenate(ws, axis=1)), carry


def _sb_prompt_kernel(q_ref, kb_scr, vb_scr, o_ref, tri_scr, *, blk):
    i = pl.program_id(2)

    @pl.when(i == 0)
    def _():
        tri_scr[...] = _neg_tri(blk)

    qs = (q_ref[...] * (1.0 / math.sqrt(HEAD_DIM))).astype(BF16)

    def block_rows(j):
        return pl.ds(pl.multiple_of(j * blk, blk), blk)

    def sweep(j_lo, n, co, diag_last, zs=None):
        carry, o = co
        rows = [block_rows(j_lo + a) for a in range(n)]
        masks = [_strict_lower(blk, blk) if diag_last and a == n - 1 else None for a in range(n)]
        if zs is None:
            zs = [_sb_z(qs, kb_scr[r, :]) for r in rows]
        cs = [jnp.dot(_sb_cumsum_lhs(z, blk, m), tri_scr[...], preferred_element_type=F32)
              for z, m in zip(zs, masks)]
        ws = []
        for a in range(n - 1, -1, -1):
            w, carry = _sb_weights(zs[a], cs[a], carry, blk, masks[a])
            ws.append(w)
        if n == 1:
            return carry, o + jnp.dot(ws[0], vb_scr[rows[0], :], preferred_element_type=F32)
        v_desc = jnp.concatenate([vb_scr[rows[a], :] for a in range(n - 1, -1, -1)], axis=0)
        return carry, o + jnp.dot(jnp.concatenate(ws, axis=1), v_desc, preferred_element_type=F32)

    co = (jnp.zeros((blk, 1), F32), jnp.zeros((blk, HEAD_DIM), F32))
    rem = i % SB_GROUP
    co = lax.switch(rem, [functools.partial(lambda r, co: sweep(i - r, r + 1, co, True), r)
                          for r in range(SB_GROUP)], co)
    co = lax.fori_loop(0, i // SB_GROUP,
                       lambda p, co: sweep(i - rem - SB_GROUP * (p + 1), SB_GROUP, co, False), co)
    o_ref[...] = co[1]


def _sb_prompt(proj, kb, vb, B, S):
    D = N_HEADS * HEAD_DIM
    t_all = proj.shape[1]
    blk = _pick_tile(S, SB_BLOCK)
    nq = S // blk
    return pl.pallas_call(
        functools.partial(_sb_prompt_kernel, blk=blk),
        grid=(B, N_HEADS, nq),
        in_specs=[pl.BlockSpec((None, blk, HEAD_DIM), lambda b, h, i: (P_Q, b * nq + i, h)),
                  pl.BlockSpec((S, HEAD_DIM), lambda b, h, i: (b, h)),
                  pl.BlockSpec((S, HEAD_DIM), lambda b, h, i: (b, h))],
        out_specs=pl.BlockSpec((blk, HEAD_DIM), lambda b, h, i: (b * nq + i, h)),
        out_shape=jax.ShapeDtypeStruct((t_all, D), F32),
        scratch_shapes=[pltpu.VMEM((blk, blk), BF16)],
        compiler_params=_params("parallel", "parallel", "arbitrary"),
        name="sb_prompt",
    )(proj, kb, vb)


def _sb_sample_kernel(q_ref, k_ref, v_ref, pk_ref, pv_ref, o_in_ref, o_ref, *, pblk):
    del o_in_ref
    T = q_ref.shape[0]
    P = pk_ref.shape[0] // N_HEADS

    def head_rows(ref, h):
        return ref[pl.ds(h, P, stride=N_HEADS), :].astype(BF16)

    heads = range(N_HEADS)
    cols = [slice(h * HEAD_DIM, (h + 1) * HEAD_DIM) for h in heads]
    mask = _strict_lower(T, T)
    qs = [(q_ref[:, cols[h]] * (1.0 / math.sqrt(HEAD_DIM))).astype(BF16) for h in heads]
    z_new = [_sb_z(qs[h], k_ref[:, cols[h]]) for h in heads]
    z_past = [_sb_z(qs[h], head_rows(pk_ref, h)) for h in heads]
    lhs_new = jnp.concatenate([_sb_cumsum_lhs(z_new[h], T, mask) for h in heads], axis=0)
    lhs_past = jnp.concatenate([_sb_cumsum_lhs(z_past[h], pblk, None) for h in heads], axis=0)
    c_new = jnp.dot(lhs_new, _neg_tri(T), preferred_element_type=F32)
    c_past = jnp.dot(lhs_past, _neg_tri(pblk), preferred_element_type=F32)
    rows_past = (P // pblk) * T
    for h in heads:
        w_new, carry = _sb_weights(z_new[h], c_new[h * T:(h + 1) * T], jnp.zeros((T, 1), F32), T, mask)
        w_past, _ = _sb_weights(z_past[h], c_past[h * rows_past:(h + 1) * rows_past], carry, pblk, None)
        o_ref[:, cols[h]] = (jnp.dot(w_new, v_ref[:, cols[h]], preferred_element_type=F32)
                             + jnp.dot(w_past, head_rows(pv_ref, h), preferred_element_type=F32))


def _sb_sample(proj, kb, vb, cache_k, cache_v, o_sb, layer, row0, B, T):
    D = N_HEADS * HEAD_DIM
    depth, _, P = cache_k.shape[:3]
    pblk = _pick_tile(P, SB_BLOCK)
    rb = row0 // T
    cache_k = cache_k.reshape(depth, B, P * N_HEADS, HEAD_DIM)
    cache_v = cache_v.reshape(depth, B, P * N_HEADS, HEAD_DIM)
    cache_spec = pl.BlockSpec((None, None, P * N_HEADS, HEAD_DIM), lambda b: (layer, b, 0, 0))
    return pl.pallas_call(
        functools.partial(_sb_sample_kernel, pblk=pblk),
        grid=(B,),
        in_specs=[pl.BlockSpec((None, T, D), lambda b: (P_Q, rb + b, 0)),
                  pl.BlockSpec((T, D), lambda b: (b, 0)),
                  pl.BlockSpec((T, D), lambda b: (b, 0)),
                  cache_spec, cache_spec,
                  pl.BlockSpec(memory_space=pl.ANY)],
        out_specs=pl.BlockSpec((T, D), lambda b: (rb + b, 0)),
        out_shape=jax.ShapeDtypeStruct(o_sb.shape, F32),
        input_output_aliases={5: 0},
        compiler_params=_params("parallel"),
        name="sb_sample",
    )(proj, kb, vb, cache_k, cache_v, o_sb)


def _cumsum_rows(x):
    n = x.shape[0]
    h1 = x.astype(BF16)
    r1 = x - h1.astype(F32)
    h2 = r1.astype(BF16)
    h3 = (r1 - h2.astype(F32)).astype(BF16)
    row = lax.broadcasted_iota(jnp.int32, (n, 3 * n), 0)
    col = lax.broadcasted_iota(jnp.int32, (n, 3 * n), 1)
    col = jnp.where(col >= 2 * n, col - 2 * n, jnp.where(col >= n, col - n, col))
    tril3 = jnp.where(row >= col, 1.0, 0.0).astype(BF16)
    return jnp.dot(tril3, jnp.concatenate([h1, h2, h3], axis=0), preferred_element_type=F32)


def _hgrn_kernel(*refs, layer, has_s0):
    if has_s0:
        q_ref, f_ref, i_ref, g_ref, lbl_ref, norm_ref, s0_ref, _, o_ref, sf_ref, st_scr = refs
    else:
        q_ref, f_ref, i_ref, g_ref, lbl_ref, norm_ref, o_ref, sf_ref, st_scr = refs
    c = pl.program_id(1)
    C = q_ref.shape[0]
    n_sub = C // HG_SUB

    @pl.when(c == 0)
    def _():
        for h in range(N_HEADS):
            if has_s0:
                st_scr[h] = s0_ref[h].T
            else:
                st_scr[h] = jnp.zeros((HEAD_DIM, HEAD_DIM), F32)

    lg = lbl_ref[...]
    ex = jnp.exp(lg - jnp.max(lg, axis=0, keepdims=True))
    soft = ex / jnp.sum(ex, axis=0, keepdims=True)
    lb = jnp.zeros((1, lg.shape[1]), F32)
    for m in range(1, layer + 1):
        lb = lb + soft[m:m + 1, :]

    a = f_ref[...]
    u = jnp.exp(-jnp.abs(a))
    lbp = jnp.maximum(lb, LB_FLOOR)
    num = jnp.where(a >= 0.0, 1.0 + lbp * u, u + lbp)
    logf = jnp.minimum(jnp.log(num) - jnp.log(1.0 + u), 0.0)
    kk_all = 1.0 - jnp.exp(logf)
    b_all = _cumsum_rows(logf)
    qraw = q_ref[...]
    q_all = qraw * _sigmoid(qraw)
    graw = g_ref[...]
    gate_all = graw * _sigmoid(graw)
    v_all = i_ref[...]

    heads = range(N_HEADS)
    cols = [slice(h * HEAD_DIM, (h + 1) * HEAD_DIM) for h in heads]
    o_inter, scores, vbs = [], [], []
    for h in heads:
        b = b_all[:, cols[h]]
        q = q_all[:, cols[h]]
        kk = kk_all[:, cols[h]]
        vb = v_all[:, cols[h]].astype(BF16)
        st = st_scr[h]
        qt = (q * jnp.exp(b)).astype(BF16)
        o_inter.append(lax.dot_general(qt, st.astype(BF16), (((1,), (1,)), ((), ())),
                                       preferred_element_type=F32))
        sc_h = []
        for i in range(n_sub):
            lo_r, hi_r = i * HG_SUB, (i + 1) * HG_SUB
            if i == 0:
                ref_row = jnp.zeros((1, HEAD_DIM), F32)
            else:
                ref_row = b[lo_r - 1:lo_r, :]
            qh = (q[lo_r:hi_r] * jnp.exp(b[lo_r:hi_r] - ref_row)).astype(BF16)
            kh = (kk[0:hi_r] * jnp.exp(ref_row - b[0:hi_r])).astype(BF16)
            sc_h.append(lax.dot_general(qh, kh, (((1,), (1,)), ((), ())),
                                        preferred_element_type=F32))
        b_last = b[C - 1:C, :]
        ke = (kk * jnp.exp(b_last - b)).astype(BF16)
        upd = lax.dot_general(vb, ke, (((0,), (0,)), ((), ())), preferred_element_type=F32)
        st_scr[h] = st * jnp.exp(b_last) + upd
        scores.append(sc_h)
        vbs.append(vb)
    for h in heads:
        parts = []
        for i in range(n_sub):
            lo_r, hi_r = i * HG_SUB, (i + 1) * HG_SUB
            tpos = lax.broadcasted_iota(jnp.int32, (HG_SUB, hi_r), 0) + lo_r
            spos = lax.broadcasted_iota(jnp.int32, (HG_SUB, hi_r), 1)
            sc = jnp.where(spos <= tpos, scores[h][i], 0.0).astype(BF16)
            parts.append(jnp.dot(sc, vbs[h][0:hi_r], preferred_element_type=F32))
        o = o_inter[h] + jnp.concatenate(parts, axis=0)
        ms = jnp.mean(o * o, axis=-1, keepdims=True)
        o_ref[:, cols[h]] = o * lax.rsqrt(ms + RMS_EPS) * norm_ref[...] * gate_all[:, cols[h]]

    @pl.when(c == pl.num_programs(1) - 1)
    def _():
        for h in range(N_HEADS):
            sf_ref[h] = st_scr[h].T


def _hgrn(proj, lb_logits, hg_norm, s0, o_prev, layer, row0, B, T):
    D = N_HEADS * HEAD_DIM
    C = HG_CHUNK
    assert T % C == 0 and row0 % C == 0
    nc = T // C
    rb = row0 // C
    depth = lb_logits.shape[0]
    t_all = proj.shape[1]

    def act_spec(split):
        return pl.BlockSpec((None, C, D), lambda b, c: (split, rb + b * nc + c, 0))

    in_specs = [act_spec(P_HQ), act_spec(P_HF), act_spec(P_HI), act_spec(P_HG),
                pl.BlockSpec((depth, D), lambda b, c: (0, 0)),
                pl.BlockSpec((1, HEAD_DIM), lambda b, c: (0, 0))]
    args = [proj, proj, proj, proj, lb_logits, hg_norm.reshape(1, HEAD_DIM)]
    if s0 is not None:
        in_specs.append(pl.BlockSpec((None, N_HEADS, HEAD_DIM, HEAD_DIM), lambda b, c: (b, 0, 0, 0)))
        in_specs.append(pl.BlockSpec(memory_space=pl.ANY))
        args += [s0, o_prev]
    return pl.pallas_call(
        functools.partial(_hgrn_kernel, layer=layer, has_s0=s0 is not None),
        grid=(B, nc),
        in_specs=in_specs,
        out_specs=[pl.BlockSpec((C, D), lambda b, c: (rb + b * nc + c, 0)),
                   pl.BlockSpec((None, N_HEADS, HEAD_DIM, HEAD_DIM), lambda b, c: (b, 0, 0, 0))],
        out_shape=[jax.ShapeDtypeStruct((t_all, D), F32),
                   jax.ShapeDtypeStruct((B, N_HEADS, HEAD_DIM, HEAD_DIM), F32)],
        input_output_aliases={} if s0 is None else {len(args) - 1: 0},
        scratch_shapes=[pltpu.VMEM((N_HEADS, HEAD_DIM, HEAD_DIM), F32)],
        compiler_params=_params("parallel", "arbitrary"),
        name="hgrn2",
    )(*args)


def _merge_out_kernel(ga_ref, gb_ref, oa_ref, ob_ref, x_ref, w_ref, g_ref, b_ref, y_ref, yb_ref, *, alpha):
    merged = _sigmoid(ga_ref[...]) * oa_ref[...] + _sigmoid(gb_ref[...]) * ob_ref[...]
    r = alpha * x_ref[...] + jnp.dot(merged.astype(BF16), w_ref[...], preferred_element_type=F32)
    y = _layer_norm(r, g_ref[...], b_ref[...])
    y_ref[...] = y
    yb_ref[...] = y.astype(BF16)


def _merge_out(proj, o_sb, o_hg, x, w_out_b, g, b, alpha):
    T, D = x.shape
    tm = _pick_tile(T, 256)
    row = pl.BlockSpec((tm, D), lambda i: (i, 0))
    vec = pl.BlockSpec((1, D), lambda i: (0, 0))
    return pl.pallas_call(
        functools.partial(_merge_out_kernel, alpha=alpha),
        grid=(T // tm,),
        in_specs=[pl.BlockSpec((None, tm, D), lambda i: (P_GA, i, 0)),
                  pl.BlockSpec((None, tm, D), lambda i: (P_GB, i, 0)),
                  row, row, row,
                  pl.BlockSpec((D, D), lambda i: (0, 0)), vec, vec],
        out_specs=[row, row],
        out_shape=[jax.ShapeDtypeStruct((T, D), F32), jax.ShapeDtypeStruct((T, D), BF16)],
        compiler_params=_params("parallel"),
        name="merge_out",
    )(proj, proj, o_sb, o_hg, x, w_out_b, g.reshape(1, D), b.reshape(1, D))


def _router_kernel(x_ref, wr_ref, bias_ref, wc_ref, wo_ref, pos_ref, cnt_ref):
    tm = x_ref.shape[0]
    logits = lax.dot_general(wr_ref[...], x_ref[...], (((1,), (1,)), ((), ())),
                             precision=lax.Precision.HIGHEST, preferred_element_type=F32)
    s = _sigmoid(logits)
    sb = s + bias_ref[...]
    sub = lax.broadcasted_iota(jnp.int32, (GROUP_SIZE, tm), 0)
    gs = []
    for g in range(N_GROUPS):
        blk = sb[g * GROUP_SIZE:(g + 1) * GROUP_SIZE, :]
        m1 = jnp.max(blk, axis=0, keepdims=True)
        first = jnp.min(jnp.where(blk == m1, sub, GROUP_SIZE), axis=0, keepdims=True)
        m2 = jnp.max(jnp.where(sub == first, -jnp.inf, blk), axis=0, keepdims=True)
        gs.append(m1 + m2)
    masked = []
    for g in range(N_GROUPS):
        beat = jnp.zeros((1, tm), jnp.int32)
        for g2 in range(N_GROUPS):
            if g2 == g:
                continue
            if g2 < g:
                beat = beat + (gs[g2] >= gs[g]).astype(jnp.int32)
            else:
                beat = beat + (gs[g2] > gs[g]).astype(jnp.int32)
        keep = beat < TOPK_GROUPS
        masked.append(jnp.where(keep, sb[g * GROUP_SIZE:(g + 1) * GROUP_SIZE, :], NEG_BIG))
    m = jnp.concatenate(masked, axis=0)
    erow = lax.broadcasted_iota(jnp.int32, (N_EXPERTS, tm), 0)
    sel = jnp.zeros((N_EXPERTS, tm), jnp.bool_)
    for _ in range(TOP_K):
        mx = jnp.max(m, axis=0, keepdims=True)
        first = jnp.min(jnp.where(m == mx, erow, N_EXPERTS), axis=0, keepdims=True)
        hit = erow == first
        sel = jnp.logical_or(sel, hit)
        m = jnp.where(hit, -jnp.inf, m)
    w = jnp.where(sel, s, 0.0)
    w = w / jnp.sum(w, axis=0, keepdims=True) * ROUTE_SCALE
    sel_f = jnp.where(sel, 1.0, 0.0)
    earlier = jnp.where(lax.broadcasted_iota(jnp.int32, (tm, tm), 0)
                        < lax.broadcasted_iota(jnp.int32, (tm, tm), 1), 1.0, 0.0).astype(BF16)
    slot = jnp.dot(sel_f.astype(BF16), earlier, preferred_element_type=F32)
    fits = jnp.logical_and(sel, slot < MOE_CAP)
    zpad = jnp.zeros((WC_LANES - N_EXPERTS, tm), F32)
    wc_ref[...] = jnp.concatenate([jnp.where(fits, w, 0.0), zpad], axis=0).T
    wo_ref[...] = jnp.concatenate([jnp.where(fits, 0.0, w), zpad], axis=0).T
    pos_ref[...] = jnp.where(fits, slot, -1.0)
    cnt_ref[...] = jnp.broadcast_to(jnp.sum(sel_f, axis=1, keepdims=True), (N_EXPERTS, WC_LANES))


def _router(x, w_router, bias):
    T, D = x.shape
    tm = MOE_TILE
    assert T % tm == 0
    nt = T // tm
    return pl.pallas_call(
        _router_kernel,
        grid=(nt,),
        in_specs=[pl.BlockSpec((tm, D), lambda i: (i, 0)),
                  pl.BlockSpec((N_EXPERTS, D), lambda i: (0, 0)),
                  pl.BlockSpec((N_EXPERTS, 1), lambda i: (0, 0))],
        out_specs=[pl.BlockSpec((tm, WC_LANES), lambda i: (i, 0)),
                   pl.BlockSpec((tm, WC_LANES), lambda i: (i, 0)),
                   pl.BlockSpec((None, N_EXPERTS, tm), lambda i: (i, 0, 0)),
                   pl.BlockSpec((None, N_EXPERTS, WC_LANES), lambda i: (i, 0, 0))],
        out_shape=[jax.ShapeDtypeStruct((T, WC_LANES), F32),
                   jax.ShapeDtypeStruct((T, WC_LANES), F32),
                   jax.ShapeDtypeStruct((nt, N_EXPERTS, tm), F32),
                   jax.ShapeDtypeStruct((nt, N_EXPERTS, WC_LANES), F32)],
        compiler_params=_params("parallel"),
        name="router",
    )(x, w_router.T, bias.reshape(N_EXPERTS, 1))


def _slot_one_hot(pos_ref, e0, n, ncap, tm):
    slot = lax.broadcasted_iota(jnp.int32, (ncap, tm), 0).astype(F32)
    return jnp.concatenate([jnp.where(pos_ref[e:e + 1, :] == slot, 1.0, 0.0).astype(BF16)
                            for e in range(e0, e0 + n)], axis=0)


def _when_slots_used(used, body):
    @pl.when(used <= MOE_CAP_FAST)
    def _():
        body(MOE_CAP_FAST)

    @pl.when(used > MOE_CAP_FAST)
    def _():
        body(MOE_CAP)


def _dispatch_kernel(used_ref, xb_ref, wc_ref, pos_ref, xs_ref):
    tm = xb_ref.shape[0]
    wc = wc_ref[...]
    hi = wc.astype(BF16)
    lo = (wc - hi.astype(F32)).astype(BF16)
    x_aug = jnp.concatenate([xb_ref[...], hi[:, :N_EXPERTS], lo[:, :N_EXPERTS]], axis=1)

    def run(ncap):
        for g in range(N_EXPERTS // MOE_EGROUP):
            p = _slot_one_hot(pos_ref, g * MOE_EGROUP, MOE_EGROUP, ncap, tm)
            rows = jnp.dot(p, x_aug, preferred_element_type=F32).astype(BF16)
            for k in range(MOE_EGROUP):
                r0 = (g * MOE_EGROUP + k) * MOE_CAP
                xs_ref[r0:r0 + ncap, :] = rows[k * ncap:(k + 1) * ncap, :]
                if ncap < MOE_CAP:
                    xs_ref[r0 + ncap:r0 + MOE_CAP, :] = jnp.zeros((MOE_CAP - ncap, x_aug.shape[1]), BF16)

    _when_slots_used(used_ref[pl.program_id(0)], run)


def _dispatch(xb, wc, pos, used):
    T, D = xb.shape
    nt = T // MOE_TILE
    return pl.pallas_call(
        _dispatch_kernel,
        grid_spec=pltpu.PrefetchScalarGridSpec(
            num_scalar_prefetch=1,
            grid=(nt,),
            in_specs=[pl.BlockSpec((MOE_TILE, D), lambda i, u: (i, 0)),
                      pl.BlockSpec((MOE_TILE, WC_LANES), lambda i, u: (i, 0)),
                      pl.BlockSpec((None, N_EXPERTS, MOE_TILE), lambda i, u: (i, 0, 0))],
            out_specs=pl.BlockSpec((None, N_EXPERTS * MOE_CAP, D + WC_LANES), lambda i, u: (i, 0, 0))),
        out_shape=jax.ShapeDtypeStruct((nt, N_EXPERTS * MOE_CAP, D + WC_LANES), BF16),
        compiler_params=_params("parallel"),
        name="moe_dispatch",
    )(used, xb, wc, pos)


def _swiglu_rows(x, col, wg_ref, wu_ref, wd_ref):
    hg = jnp.dot(x, wg_ref[...].astype(BF16), preferred_element_type=F32)
    hu = jnp.dot(x, wu_ref[...].astype(BF16), preferred_element_type=F32)
    h = hg * _sigmoid(hg) * hu
    if col is not None:
        h = h * col
    return jnp.dot(h.astype(BF16), wd_ref[...].astype(BF16), preferred_element_type=F32)


def _experts_kernel(used_ref, xs_ref, wg32_ref, wu32_ref, wd32_ref, ys_ref, wg_ref, wu_ref, wd_ref):
    e = pl.program_id(0)
    j = pl.program_id(1)

    @pl.when(j == 0)
    def _():
        wg_ref[...] = wg32_ref[...].astype(BF16)
        wu_ref[...] = wu32_ref[...].astype(BF16)
        wd_ref[...] = wd32_ref[...].astype(BF16)

    tb, cap, width = xs_ref.shape
    d = width - WC_LANES

    def run(ncap):
        xa = xs_ref[:, 0:ncap, :].reshape(tb * ncap, width)
        aug = xa[:, d:].astype(F32)
        lane = lax.broadcasted_iota(jnp.int32, aug.shape, 1)
        mine = jnp.logical_or(lane == e, lane == e + N_EXPERTS)
        col = jnp.sum(jnp.where(mine, aug, 0.0), axis=1, keepdims=True)
        y = _swiglu_rows(xa[:, :d], col, wg_ref, wu_ref, wd_ref)
        ys_ref[:, 0:ncap, :] = y.astype(BF16).reshape(tb, ncap, d)
        if ncap < cap:
            ys_ref[:, ncap:cap, :] = jnp.zeros((tb, cap - ncap, d), BF16)

    _when_slots_used(used_ref[e * pl.num_programs(1) + j], run)


def _experts(xs, cnt, wg, wu, wd, layer):
    nt, _, width = xs.shape
    _, n_e, D, F = wg.shape
    tb = _pick_tile(nt, MOE_TILES_PER_STEP) if nt > MOE_TILES_PER_STEP else nt
    if nt % tb:
        tb = 1
    steps = nt // tb
    used = jnp.max(cnt.reshape(steps, tb, n_e), axis=1).T.reshape(n_e * steps).astype(jnp.int32)
    return pl.pallas_call(
        _experts_kernel,
        grid_spec=pltpu.PrefetchScalarGridSpec(
            num_scalar_prefetch=1,
            grid=(n_e, steps),
            in_specs=[pl.BlockSpec((tb, MOE_CAP, width), lambda e, j, u: (j, e, 0)),
                      pl.BlockSpec((None, None, D, F), lambda e, j, u: (layer, e, 0, 0)),
                      pl.BlockSpec((None, None, D, F), lambda e, j, u: (layer, e, 0, 0)),
                      pl.BlockSpec((None, None, F, D), lambda e, j, u: (layer, e, 0, 0))],
            out_specs=pl.BlockSpec((tb, MOE_CAP, D), lambda e, j, u: (j, e, 0)),
            scratch_shapes=[pltpu.VMEM((D, F), BF16), pltpu.VMEM((D, F), BF16),
                            pltpu.VMEM((F, D), BF16)]),
        out_shape=jax.ShapeDtypeStruct((nt, n_e * MOE_CAP, D), BF16),
        compiler_params=_params("parallel", "arbitrary"),
        name="moe_experts",
    )(used, xs, wg, wu, wd)


def _combine_kernel(used_ref, ys_ref, pos_ref, x_ref, xb_ref, yo_ref, sg_ref, su_ref, sd_ref, g_ref,
                    b_ref, y_ref, yb_ref, *, alpha, t0):
    tm = x_ref.shape[0]

    def run(ncap):
        acc = yo_ref[...] + _swiglu_rows(xb_ref[...], None, sg_ref, su_ref, sd_ref)
        for g in range(N_EXPERTS // MOE_EGROUP):
            p = _slot_one_hot(pos_ref, g * MOE_EGROUP, MOE_EGROUP, ncap, tm)
            ys = jnp.concatenate([ys_ref[e * MOE_CAP:e * MOE_CAP + ncap, :]
                                  for e in range(g * MOE_EGROUP, (g + 1) * MOE_EGROUP)], axis=0)
            acc = acc + lax.dot_general(p, ys, (((0,), (0,)), ((), ())), preferred_element_type=F32)
        y = _layer_norm(alpha * x_ref[...] + acc, g_ref[...], b_ref[...])
        y_ref[...] = y
        yb_ref[...] = y.astype(BF16)

    _when_slots_used(used_ref[t0 + pl.program_id(0)], run)


def _combine(ys, pos, used, x, xb, y_over, sg, su, sd, g, b, alpha, layer, row0, n):
    D = x.shape[1]
    F = sg.shape[2]
    assert row0 % MOE_TILE == 0 and n % MOE_TILE == 0
    t0 = row0 // MOE_TILE
    row_in = pl.BlockSpec((MOE_TILE, D), lambda i, u: (t0 + i, 0))
    row_out = pl.BlockSpec((MOE_TILE, D), lambda i, u: (i, 0))
    vec = pl.BlockSpec((1, D), lambda i, u: (0, 0))
    return pl.pallas_call(
        functools.partial(_combine_kernel, alpha=alpha, t0=t0),
        grid_spec=pltpu.PrefetchScalarGridSpec(
            num_scalar_prefetch=1,
            grid=(n // MOE_TILE,),
            in_specs=[pl.BlockSpec((None, N_EXPERTS * MOE_CAP, D), lambda i, u: (t0 + i, 0, 0)),
                      pl.BlockSpec((None, N_EXPERTS, MOE_TILE), lambda i, u: (t0 + i, 0, 0)),
                      row_in, row_in, row_in,
                      pl.BlockSpec((None, D, F), lambda i, u: (layer, 0, 0)),
                      pl.BlockSpec((None, D, F), lambda i, u: (layer, 0, 0)),
                      pl.BlockSpec((None, F, D), lambda i, u: (layer, 0, 0)),
                      vec, vec],
            out_specs=[row_out, row_out]),
        out_shape=[jax.ShapeDtypeStruct((n, D), F32), jax.ShapeDtypeStruct((n, D), BF16)],
        compiler_params=_params("parallel"),
        name="moe_combine",
    )(used, ys, pos, x, xb, y_over, sg, su, sd, g.reshape(1, D), b.reshape(1, D))


def _dense_experts_kernel(xb_ref, wo_ref, wg_ref, wu_ref, wd_ref, y_ref):
    e = pl.program_id(1)

    @pl.when(e == 0)
    def _():
        y_ref[...] = jnp.zeros_like(y_ref)

    lane = lax.broadcasted_iota(jnp.int32, wo_ref.shape, 1)
    col = jnp.sum(jnp.where(lane == e, wo_ref[...], 0.0), axis=1, keepdims=True)
    y_ref[...] += _swiglu_rows(xb_ref[...], col, wg_ref, wu_ref, wd_ref)


def _dense_experts(xb, wo, wg, wu, wd, layer):
    T, D = xb.shape
    _, n_e, _, F = wg.shape
    tm = _pick_tile(T, 1024)
    return pl.pallas_call(
        _dense_experts_kernel,
        grid=(T // tm, n_e),
        in_specs=[pl.BlockSpec((tm, D), lambda i, e: (i, 0)),
                  pl.BlockSpec((tm, WC_LANES), lambda i, e: (i, 0)),
                  pl.BlockSpec((None, None, D, F), lambda i, e: (layer, e, 0, 0)),
                  pl.BlockSpec((None, None, D, F), lambda i, e: (layer, e, 0, 0)),
                  pl.BlockSpec((None, None, F, D), lambda i, e: (layer, e, 0, 0))],
        out_specs=pl.BlockSpec((tm, D), lambda i, e: (i, 0)),
        out_shape=jax.ShapeDtypeStruct((T, D), F32),
        compiler_params=_params("parallel", "arbitrary"),
        name="moe_dense_overflow",
    )(xb, wo, wg, wu, wd)


def _moe(xb, x, w_router, bias, wg, wu, wd, sg, su, sd, g, b, alpha, layer, ranges):
    wc, wo, pos, cnt = _router(x, w_router, bias)
    cnt = cnt[:, :, 0]
    tile_used = jnp.max(cnt, axis=1).astype(jnp.int32)
    ys = _experts(_dispatch(xb, wc, pos, tile_used), cnt, wg, wu, wd, layer)
    y_over = lax.cond(jnp.max(cnt) > MOE_CAP,
                      lambda: _dense_experts(xb, wo, wg, wu, wd, layer),
                      lambda: jnp.zeros(x.shape, F32))
    return [_combine(ys, pos, tile_used, x, xb, y_over, sg, su, sd, g, b, alpha, layer, row0, n)
            for row0, n in ranges]


def kernel(x_prompt, x_sample, cache_sb_k, cache_sb_v, state_hgrn, ln_in_g, ln_in_b, w_in, w_out, hg_norm, hg_lb_logits, ln1_g, ln1_b, w_router, router_bias, w_exp_gate, w_exp_up, w_exp_down, w_sh_gate, w_sh_up, w_sh_down, ln2_g, ln2_b):
    B, S, D = x_prompt.shape
    Bs, Ts, _ = x_sample.shape
    depth = w_in.shape[0]
    P = cache_sb_k.shape[2]
    n_p = B * S
    alpha = (2 * depth) ** 0.25

    x_all = jnp.concatenate([x_prompt.reshape(n_p, D), x_sample.reshape(Bs * Ts, D)], axis=0)
    x, xb = _entry_ln(x_all, ln_in_g, ln_in_b)

    sp, sd = [], []
    kp = vp = kd = vd = None
    for l in range(depth):
        w_in_b = w_in[l].astype(BF16)
        proj = _in_proj(xb, w_in_b)
        kb_p, vb_p, kp, vp = _kv_proj(xb, w_in_b, kp, vp, l, depth, 0, n_p)
        kb_s, vb_s, kd, vd = _kv_proj(xb, w_in_b, kd, vd, l, depth, n_p, Bs * Ts)
        o_sb = _sb_prompt(proj, kb_p, vb_p, B, S)
        o_sb = _sb_sample(proj, kb_s, vb_s, cache_sb_k, cache_sb_v, o_sb, l, n_p, Bs, Ts)
        o_hg, s_p = _hgrn(proj, hg_lb_logits, hg_norm[l], None, None, l, 0, B, S)
        o_hg, s_s = _hgrn(proj, hg_lb_logits, hg_norm[l], state_hgrn[l], o_hg, l, n_p, Bs, Ts)
        x, xb = _merge_out(proj, o_sb, o_hg, x, w_out[l].astype(BF16), ln1_g[l], ln1_b[l], alpha)
        last = l == depth - 1
        ranges = [(0, n_p), (n_p, Bs * Ts)] if last else [(0, n_p + Bs * Ts)]
        outs = _moe(xb, x, w_router[l], router_bias[l], w_exp_gate, w_exp_up, w_exp_down,
                    w_sh_gate, w_sh_up, w_sh_down, ln2_g[l], ln2_b[l], alpha, l, ranges)
        if not last:
            x, xb = outs[0]
        sp.append(s_p)
        sd.append(s_s)
    kv_p = (depth, B, S, N_HEADS, HEAD_DIM)
    kv_s = (depth, Bs, Ts, N_HEADS, HEAD_DIM)
    return (outs[0][0].reshape(B, S, D), outs[1][0].reshape(Bs, Ts, D),
            kp.reshape(kv_p), vp.reshape(kv_p), jnp.stack(sp),
            kd.reshape(kv_s), vd.reshape(kv_s), jnp.stack(sd))
```

```python
import functools
import math

import jax
import jax.numpy as jnp
from jax import lax
from jax.experimental import pallas as pl
from jax.experimental.pallas import tpu as pltpu

F32 = jnp.float32
BF16 = jnp.bfloat16

N_HEADS = 8
HEAD_DIM = 128
N_EXPERTS = 64
TOP_K = 8
N_GROUPS = 8
TOPK_GROUPS = 4
GROUP_SIZE = N_EXPERTS // N_GROUPS
ROUTE_SCALE = 2.5
NEG_BIG = -1e9
LB_FLOOR = 1e-30
LN_EPS = 1e-5
RMS_EPS = 1e-6
LOG2E = 1.4426950408889634
N_SPLITS = 9
P_Q, P_HQ, P_HF, P_HI, P_HG, P_GA, P_GB = range(7)

LANES = 128
SUBLANES = 8
VMEM_LIMIT = 48 * 1024 * 1024

HG_CHUNK = 128
HG_SUB = 32
SB_BLOCK = 256
SB_GROUP = 8
WC_LANES = 128
MOE_TILE = 256
ROUTER_TILES = 2
MOE_CAP = 64
MOE_CAP_FAST = 48
MOE_EGROUP = 16
MOE_TILES_PER_STEP = 24


def _pick_tile(n, pref):
    if n <= pref:
        return n
    for t in range(pref, 7, -1):
        if n % t == 0 and t % SUBLANES == 0:
            return t
    raise ValueError(f"no tile for {n}")


def _params(*sem):
    return pltpu.CompilerParams(dimension_semantics=sem, vmem_limit_bytes=VMEM_LIMIT)


def _layer_norm(x, g, b):
    mu = jnp.mean(x, axis=-1, keepdims=True)
    xc = x - mu
    var = jnp.mean(xc * xc, axis=-1, keepdims=True)
    return xc * lax.rsqrt(var + LN_EPS) * g + b


def _sigmoid(x):
    return 1.0 / (1.0 + jnp.exp(-x))


def _log_sigmoid(x):
    return jnp.minimum(x, 0.0) - jnp.log(1.0 + jnp.exp(-jnp.abs(x)))


def _ln_kernel(xp_ref, xs_ref, g_ref, b_ref, y_ref, yb_ref, *, p_tiles):
    x = jnp.where(pl.program_id(0) < p_tiles, xp_ref[...], xs_ref[...])
    y = _layer_norm(x, g_ref[...], b_ref[...])
    y_ref[...] = y
    yb_ref[...] = y.astype(BF16)


def _entry_ln(xp, xs, g, b):
    n_p, D = xp.shape
    n_s = xs.shape[0]
    tm = _pick_tile(math.gcd(n_p, n_s), 512)
    p_tiles = n_p // tm
    T = n_p + n_s
    return pl.pallas_call(
        functools.partial(_ln_kernel, p_tiles=p_tiles),
        grid=(T // tm,),
        in_specs=[pl.BlockSpec((tm, D), lambda i: (jnp.minimum(i, p_tiles - 1), 0)),
                  pl.BlockSpec((tm, D), lambda i: (jnp.maximum(i - p_tiles, 0), 0)),
                  pl.BlockSpec((1, D), lambda i: (0, 0)),
                  pl.BlockSpec((1, D), lambda i: (0, 0))],
        out_specs=[pl.BlockSpec((tm, D), lambda i: (i, 0)),
                   pl.BlockSpec((tm, D), lambda i: (i, 0))],
        out_shape=[jax.ShapeDtypeStruct((T, D), F32), jax.ShapeDtypeStruct((T, D), BF16)],
        compiler_params=_params("parallel"),
        name="entry_ln",
    )(xp, xs, g.reshape(1, D), b.reshape(1, D))


def _in_proj_kernel(x_ref, w_ref, o_ref):
    o_ref[...] = jnp.dot(x_ref[...], w_ref[...], preferred_element_type=F32)


def _in_proj(xb, w_in_b):
    T, D = xb.shape
    tm = _pick_tile(T, 1024)
    return pl.pallas_call(
        _in_proj_kernel,
        grid=(N_SPLITS - 2, T // tm),
        in_specs=[pl.BlockSpec((tm, D), lambda j, i: (i, 0)),
                  pl.BlockSpec((D, D), lambda j, i: (0, jnp.where(j >= 1, j + 2, j)))],
        out_specs=pl.BlockSpec((None, tm, D), lambda j, i: (j, i, 0)),
        out_shape=jax.ShapeDtypeStruct((N_SPLITS - 2, T, D), F32),
        compiler_params=_params("parallel", "parallel"),
        name="in_proj",
    )(xb, w_in_b)


def _kv_proj_kernel(x_ref, wk_ref, wv_ref, *refs):
    kb_ref, vb_ref, ko_ref, vo_ref = refs[-4:]
    tm = x_ref.shape[0]
    x = x_ref[...]
    for w_ref, b_ref, o_ref in ((wk_ref, kb_ref, ko_ref), (wv_ref, vb_ref, vo_ref)):
        y = jnp.dot(x, w_ref[...], preferred_element_type=F32)
        b_ref[...] = y.astype(BF16)
        for h in range(N_HEADS):
            o_ref[pl.ds(h, tm, stride=N_HEADS), :] = y[:, h * HEAD_DIM:(h + 1) * HEAD_DIM]


def _kv_proj(xb, w_in_b, k_prev, v_prev, layer, depth, row0, n):
    D = xb.shape[1]
    tm = _pick_tile(n, 512)
    assert row0 % tm == 0
    rb = row0 // tm
    out_spec = pl.BlockSpec((None, tm * N_HEADS, HEAD_DIM), lambda i: (layer, i, 0))
    out_sds = jax.ShapeDtypeStruct((depth, n * N_HEADS, HEAD_DIM), F32)
    args = [xb, w_in_b, w_in_b]
    in_specs = [pl.BlockSpec((tm, D), lambda i: (rb + i, 0)),
                pl.BlockSpec((D, D), lambda i: (0, 1)),
                pl.BlockSpec((D, D), lambda i: (0, 2))]
    aliases = {}
    if k_prev is not None:
        args += [k_prev, v_prev]
        in_specs += [pl.BlockSpec(memory_space=pl.ANY)] * 2
        aliases = {3: 2, 4: 3}
    return pl.pallas_call(
        _kv_proj_kernel,
        grid=(n // tm,),
        in_specs=in_specs,
        out_specs=[pl.BlockSpec((tm, D), lambda i: (i, 0)), pl.BlockSpec((tm, D), lambda i: (i, 0)),
                   out_spec, out_spec],
        out_shape=[jax.ShapeDtypeStruct((n, D), BF16), jax.ShapeDtypeStruct((n, D), BF16),
                   out_sds, out_sds],
        input_output_aliases=aliases,
        compiler_params=_params("parallel"),
        name="kv_proj",
    )(*args)


def _neg_tri(tk):
    r = lax.broadcasted_iota(jnp.int32, (tk, tk), 0)
    c = lax.broadcasted_iota(jnp.int32, (tk, tk), 1)
    return jnp.where(r >= c, -1.0, 0.0).astype(BF16)


def _strict_lower(tq, tk):
    return (lax.broadcasted_iota(jnp.int32, (tq, tk), 1)
            < lax.broadcasted_iota(jnp.int32, (tq, tk), 0))


def _sb_z(qs, kb):
    return lax.dot_general(qs, kb, (((1,), (1,)), ((), ())), preferred_element_type=F32)


def _sb_cumsum_lhs(z, tk, mask):
    sp = jnp.maximum(z, 0.0) + jnp.log(1.0 + jnp.exp2(jnp.abs(z) * (-LOG2E)))
    if mask is not None:
        sp = jnp.where(mask, sp, 0.0)
    sp = sp.astype(BF16)
    n = z.shape[1] // tk
    return sp if n == 1 else jnp.concatenate([sp[:, a * tk:(a + 1) * tk] for a in range(n)], axis=0)


def _sb_weights(z, c, carry, tk, mask):
    tq = z.shape[0]
    n = z.shape[1] // tk
    ws = [None] * n
    for a in range(n - 1, -1, -1):
        ca = c[a * tq:(a + 1) * tq, :]
        w = jnp.exp2((z[:, a * tk:(a + 1) * tk] + ca + carry) * LOG2E)
        if mask is not None:
            w = jnp.where(mask, w, 0.0)
        ws[a] = w.astype(BF16)
        carry = carry + ca[:, 0:1]
    return (ws[0] if n == 1 else jnp.concatenate(ws, axis=1)), carry


def _sb_prompt_kernel(q_ref, kb_scr, vb_scr, o_ref, tri_scr, *, blk):
    i = pl.program_id(2)

    @pl.when(i == 0)
    def _():
        tri_scr[...] = _neg_tri(blk)

    qs = (q_ref[...] * (1.0 / math.sqrt(HEAD_DIM))).astype(BF16)

    def block_rows(j):
        return pl.ds(pl.multiple_of(j * blk, blk), blk)

    def sweep(j_lo, n, co, diag_last, zs=None):
        carry, o = co
        rows = [block_rows(j_lo + a) for a in range(n)]
        masks = [_strict_lower(blk, blk) if diag_last and a == n - 1 else None for a in range(n)]
        if zs is None:
            zs = [_sb_z(qs, kb_scr[r, :]) for r in rows]
        cs = [jnp.dot(_sb_cumsum_lhs(z, blk, m), tri_scr[...], preferred_element_type=F32)
              for z, m in zip(zs, masks)]
        ws = []
        for a in range(n - 1, -1, -1):
            w, carry = _sb_weights(zs[a], cs[a], carry, blk, masks[a])
            ws.append(w)
        if n == 1:
            return carry, o + jnp.dot(ws[0], vb_scr[rows[0], :], preferred_element_type=F32)
        v_desc = jnp.concatenate([vb_scr[rows[a], :] for a in range(n - 1, -1, -1)], axis=0)
        return carry, o + jnp.dot(jnp.concatenate(ws, axis=1), v_desc, preferred_element_type=F32)

    co = (jnp.zeros((blk, 1), F32), jnp.zeros((blk, HEAD_DIM), F32))
    rem = i % SB_GROUP
    co = lax.switch(rem, [functools.partial(lambda r, co: sweep(i - r, r + 1, co, True), r)
                          for r in range(SB_GROUP)], co)
    co = lax.fori_loop(0, i // SB_GROUP,
                       lambda p, co: sweep(i - rem - SB_GROUP * (p + 1), SB_GROUP, co, False), co)
    o_ref[...] = co[1]


def _sb_prompt(proj, kb, vb, B, S):
    D = N_HEADS * HEAD_DIM
    t_all = proj.shape[1]
    blk = _pick_tile(S, SB_BLOCK)
    nq = S // blk
    return pl.pallas_call(
        functools.partial(_sb_prompt_kernel, blk=blk),
        grid=(B, N_HEADS, nq),
        in_specs=[pl.BlockSpec((None, blk, HEAD_DIM), lambda b, h, i: (P_Q, b * nq + i, h)),
                  pl.BlockSpec((S, HEAD_DIM), lambda b, h, i: (b, h)),
                  pl.BlockSpec((S, HEAD_DIM), lambda b, h, i: (b, h))],
        out_specs=pl.BlockSpec((blk, HEAD_DIM), lambda b, h, i: (b * nq + i, h)),
        out_shape=jax.ShapeDtypeStruct((t_all, D), F32),
        scratch_shapes=[pltpu.VMEM((blk, blk), BF16)],
        compiler_params=_params("parallel", "parallel", "arbitrary"),
        name="sb_prompt",
    )(proj, kb, vb)


def _sb_sample_kernel(q_ref, k_ref, v_ref, pk_ref, pv_ref, o_in_ref, o_ref, *, pblk):
    del o_in_ref
    T = q_ref.shape[0]
    P = pk_ref.shape[0] // N_HEADS

    def head_rows(ref, h):
        return ref[pl.ds(h, P, stride=N_HEADS), :].astype(BF16)

    heads = range(N_HEADS)
    cols = [slice(h * HEAD_DIM, (h + 1) * HEAD_DIM) for h in heads]
    mask = _strict_lower(T, T)
    qs = [(q_ref[:, cols[h]] * (1.0 / math.sqrt(HEAD_DIM))).astype(BF16) for h in heads]
    z_new = [_sb_z(qs[h], k_ref[:, cols[h]]) for h in heads]
    z_past = [_sb_z(qs[h], head_rows(pk_ref, h)) for h in heads]
    lhs_new = jnp.concatenate([_sb_cumsum_lhs(z_new[h], T, mask) for h in heads], axis=0)
    lhs_past = jnp.concatenate([_sb_cumsum_lhs(z_past[h], pblk, None) for h in heads], axis=0)
    c_new = jnp.dot(lhs_new, _neg_tri(T), preferred_element_type=F32)
    c_past = jnp.dot(lhs_past, _neg_tri(pblk), preferred_element_type=F32)
    rows_past = (P // pblk) * T
    for h in heads:
        w_new, carry = _sb_weights(z_new[h], c_new[h * T:(h + 1) * T], jnp.zeros((T, 1), F32), T, mask)
        w_past, _ = _sb_weights(z_past[h], c_past[h * rows_past:(h + 1) * rows_past], carry, pblk, None)
        o_ref[:, cols[h]] = (jnp.dot(w_new, v_ref[:, cols[h]], preferred_element_type=F32)
                             + jnp.dot(w_past, head_rows(pv_ref, h), preferred_element_type=F32))


def _sb_sample(proj, kb, vb, cache_k, cache_v, o_sb, layer, row0, B, T):
    D = N_HEADS * HEAD_DIM
    depth, _, P = cache_k.shape[:3]
    pblk = _pick_tile(P, SB_BLOCK)
    rb = row0 // T
    cache_k = cache_k.reshape(depth, B, P * N_HEADS, HEAD_DIM)
    cache_v = cache_v.reshape(depth, B, P * N_HEADS, HEAD_DIM)
    cache_spec = pl.BlockSpec((None, None, P * N_HEADS, HEAD_DIM), lambda b: (layer, b, 0, 0))
    return pl.pallas_call(
        functools.partial(_sb_sample_kernel, pblk=pblk),
        grid=(B,),
        in_specs=[pl.BlockSpec((None, T, D), lambda b: (P_Q, rb + b, 0)),
                  pl.BlockSpec((T, D), lambda b: (b, 0)),
                  pl.BlockSpec((T, D), lambda b: (b, 0)),
                  cache_spec, cache_spec,
                  pl.BlockSpec(memory_space=pl.ANY)],
        out_specs=pl.BlockSpec((T, D), lambda b: (rb + b, 0)),
        out_shape=jax.ShapeDtypeStruct(o_sb.shape, F32),
        input_output_aliases={5: 0},
        compiler_params=_params("parallel"),
        name="sb_sample",
    )(proj, kb, vb, cache_k, cache_v, o_sb)


def _cumsum_rows(x):
    n = x.shape[0]
    h1 = x.astype(BF16)
    r1 = x - h1.astype(F32)
    h2 = r1.astype(BF16)
    h3 = (r1 - h2.astype(F32)).astype(BF16)
    row = lax.broadcasted_iota(jnp.int32, (n, 3 * n), 0)
    col = lax.broadcasted_iota(jnp.int32, (n, 3 * n), 1)
    col = jnp.where(col >= 2 * n, col - 2 * n, jnp.where(col >= n, col - n, col))
    tril3 = jnp.where(row >= col, 1.0, 0.0).astype(BF16)
    return jnp.dot(tril3, jnp.concatenate([h1, h2, h3], axis=0), preferred_element_type=F32)


def _hgrn_kernel(*refs, layer, has_s0):
    if has_s0:
        q_ref, f_ref, i_ref, g_ref, lbl_ref, norm_ref, s0_ref, _, o_ref, sf_ref, st_scr = refs
    else:
        q_ref, f_ref, i_ref, g_ref, lbl_ref, norm_ref, o_ref, sf_ref, st_scr = refs
    c = pl.program_id(1)
    C = q_ref.shape[0]
    n_sub = C // HG_SUB

    @pl.when(c == 0)
    def _():
        for h in range(N_HEADS):
            if has_s0:
                st_scr[h] = s0_ref[h].T
            else:
                st_scr[h] = jnp.zeros((HEAD_DIM, HEAD_DIM), F32)

    lg = lbl_ref[...]
    ex = jnp.exp(lg - jnp.max(lg, axis=0, keepdims=True))
    soft = ex / jnp.sum(ex, axis=0, keepdims=True)
    lb = jnp.zeros((1, lg.shape[1]), F32)
    for m in range(1, layer + 1):
        lb = lb + soft[m:m + 1, :]

    a = f_ref[...]
    u = jnp.exp(-jnp.abs(a))
    lbp = jnp.maximum(lb, LB_FLOOR)
    num = jnp.where(a >= 0.0, 1.0 + lbp * u, u + lbp)
    logf = jnp.minimum(jnp.log(num) - jnp.log(1.0 + u), 0.0)
    kk_all = 1.0 - jnp.exp(logf)
    b_all = _cumsum_rows(logf)
    qraw = q_ref[...]
    q_all = qraw * _sigmoid(qraw)
    graw = g_ref[...]
    gate_all = graw * _sigmoid(graw)
    v_all = i_ref[...]

    heads = range(N_HEADS)
    cols = [slice(h * HEAD_DIM, (h + 1) * HEAD_DIM) for h in heads]
    o_inter, scores, vbs = [], [], []
    for h in heads:
        b = b_all[:, cols[h]]
        q = q_all[:, cols[h]]
        kk = kk_all[:, cols[h]]
        vb = v_all[:, cols[h]].astype(BF16)
        st = st_scr[h]
        qt = (q * jnp.exp(b)).astype(BF16)
        o_inter.append(lax.dot_general(qt, st.astype(BF16), (((1,), (1,)), ((), ())),
                                       preferred_element_type=F32))
        sc_h = []
        for i in range(n_sub):
            lo_r, hi_r = i * HG_SUB, (i + 1) * HG_SUB
            if i == 0:
                ref_row = jnp.zeros((1, HEAD_DIM), F32)
            else:
                ref_row = b[lo_r - 1:lo_r, :]
            qh = (q[lo_r:hi_r] * jnp.exp(b[lo_r:hi_r] - ref_row)).astype(BF16)
            kh = (kk[0:hi_r] * jnp.exp(ref_row - b[0:hi_r])).astype(BF16)
            sc_h.append(lax.dot_general(qh, kh, (((1,), (1,)), ((), ())),
                                        preferred_element_type=F32))
        b_last = b[C - 1:C, :]
        ke = (kk * jnp.exp(b_last - b)).astype(BF16)
        upd = lax.dot_general(vb, ke, (((0,), (0,)), ((), ())), preferred_element_type=F32)
        st_scr[h] = st * jnp.exp(b_last) + upd
        scores.append(sc_h)
        vbs.append(vb)
    for h in heads:
        parts = []
        for i in range(n_sub):
            lo_r, hi_r = i * HG_SUB, (i + 1) * HG_SUB
            tpos = lax.broadcasted_iota(jnp.int32, (HG_SUB, hi_r), 0) + lo_r
            spos = lax.broadcasted_iota(jnp.int32, (HG_SUB, hi_r), 1)
            sc = jnp.where(spos <= tpos, scores[h][i], 0.0).astype(BF16)
            parts.append(jnp.dot(sc, vbs[h][0:hi_r], preferred_element_type=F32))
        o = o_inter[h] + jnp.concatenate(parts, axis=0)
        ms = jnp.mean(o * o, axis=-1, keepdims=True)
        o_ref[:, cols[h]] = o * lax.rsqrt(ms + RMS_EPS) * norm_ref[...] * gate_all[:, cols[h]]

    @pl.when(c == pl.num_programs(1) - 1)
    def _():
        for h in range(N_HEADS):
            sf_ref[h] = st_scr[h].T


def _hgrn(proj, lb_logits, hg_norm, s0, o_prev, layer, row0, B, T):
    D = N_HEADS * HEAD_DIM
    C = min(HG_CHUNK, T)
    assert T % C == 0 and row0 % C == 0 and C % HG_SUB == 0
    nc = T // C
    rb = row0 // C
    depth = lb_logits.shape[0]
    t_all = proj.shape[1]

    def act_spec(split):
        return pl.BlockSpec((None, C, D), lambda b, c: (split, rb + b * nc + c, 0))

    in_specs = [act_spec(P_HQ), act_spec(P_HF), act_spec(P_HI), act_spec(P_HG),
                pl.BlockSpec((depth, D), lambda b, c: (0, 0)),
                pl.BlockSpec((1, HEAD_DIM), lambda b, c: (0, 0))]
    args = [proj, proj, proj, proj, lb_logits, hg_norm.reshape(1, HEAD_DIM)]
    if s0 is not None:
        in_specs.append(pl.BlockSpec((None, N_HEADS, HEAD_DIM, HEAD_DIM), lambda b, c: (b, 0, 0, 0)))
        in_specs.append(pl.BlockSpec(memory_space=pl.ANY))
        args += [s0, o_prev]
    return pl.pallas_call(
        functools.partial(_hgrn_kernel, layer=layer, has_s0=s0 is not None),
        grid=(B, nc),
        in_specs=in_specs,
        out_specs=[pl.BlockSpec((C, D), lambda b, c: (rb + b * nc + c, 0)),
                   pl.BlockSpec((None, N_HEADS, HEAD_DIM, HEAD_DIM), lambda b, c: (b, 0, 0, 0))],
        out_shape=[jax.ShapeDtypeStruct((t_all, D), F32),
                   jax.ShapeDtypeStruct((B, N_HEADS, HEAD_DIM, HEAD_DIM), F32)],
        input_output_aliases={} if s0 is None else {len(args) - 1: 0},
        scratch_shapes=[pltpu.VMEM((N_HEADS, HEAD_DIM, HEAD_DIM), F32)],
        compiler_params=_params("parallel", "arbitrary"),
        name="hgrn2",
    )(*args)


def _merge_out_kernel(ga_ref, gb_ref, oa_ref, ob_ref, x_ref, w_ref, g_ref, b_ref, y_ref, yb_ref, *, alpha):
    merged = _sigmoid(ga_ref[...]) * oa_ref[...] + _sigmoid(gb_ref[...]) * ob_ref[...]
    r = alpha * x_ref[...] + jnp.dot(merged.astype(BF16), w_ref[...], preferred_element_type=F32)
    y = _layer_norm(r, g_ref[...], b_ref[...])
    y_ref[...] = y
    yb_ref[...] = y.astype(BF16)


def _merge_out(proj, o_sb, o_hg, x, w_out_b, g, b, alpha):
    T, D = x.shape
    tm = _pick_tile(T, 256)
    row = pl.BlockSpec((tm, D), lambda i: (i, 0))
    vec = pl.BlockSpec((1, D), lambda i: (0, 0))
    return pl.pallas_call(
        functools.partial(_merge_out_kernel, alpha=alpha),
        grid=(T // tm,),
        in_specs=[pl.BlockSpec((None, tm, D), lambda i: (P_GA, i, 0)),
                  pl.BlockSpec((None, tm, D), lambda i: (P_GB, i, 0)),
                  row, row, row,
                  pl.BlockSpec((D, D), lambda i: (0, 0)), vec, vec],
        out_specs=[row, row],
        out_shape=[jax.ShapeDtypeStruct((T, D), F32), jax.ShapeDtypeStruct((T, D), BF16)],
        compiler_params=_params("parallel"),
        name="merge_out",
    )(proj, proj, o_sb, o_hg, x, w_out_b, g.reshape(1, D), b.reshape(1, D))


def _router_kernel(x_ref, wr_ref, bias_ref, wc_ref, wo_ref, pos_ref, cnt_ref):
    tm = x_ref.shape[0]
    logits = lax.dot_general(wr_ref[...], x_ref[...], (((1,), (1,)), ((), ())),
                             precision=lax.Precision.HIGHEST, preferred_element_type=F32)
    s = _sigmoid(logits)
    sb = s + bias_ref[...]
    sub = lax.broadcasted_iota(jnp.int32, (GROUP_SIZE, tm), 0)
    gs = []
    for g in range(N_GROUPS):
        blk = sb[g * GROUP_SIZE:(g + 1) * GROUP_SIZE, :]
        m1 = jnp.max(blk, axis=0, keepdims=True)
        first = jnp.min(jnp.where(blk == m1, sub, GROUP_SIZE), axis=0, keepdims=True)
        m2 = jnp.max(jnp.where(sub == first, -jnp.inf, blk), axis=0, keepdims=True)
        gs.append(m1 + m2)
    masked = []
    for g in range(N_GROUPS):
        beat = jnp.zeros((1, tm), jnp.int32)
        for g2 in range(N_GROUPS):
            if g2 == g:
                continue
            if g2 < g:
                beat = beat + (gs[g2] >= gs[g]).astype(jnp.int32)
            else:
                beat = beat + (gs[g2] > gs[g]).astype(jnp.int32)
        keep = beat < TOPK_GROUPS
        masked.append(jnp.where(keep, sb[g * GROUP_SIZE:(g + 1) * GROUP_SIZE, :], NEG_BIG))
    m = jnp.concatenate(masked, axis=0)
    erow = lax.broadcasted_iota(jnp.int32, (N_EXPERTS, tm), 0)
    sel = jnp.zeros((N_EXPERTS, tm), jnp.bool_)
    for _ in range(TOP_K):
        mx = jnp.max(m, axis=0, keepdims=True)
        first = jnp.min(jnp.where(m == mx, erow, N_EXPERTS), axis=0, keepdims=True)
        hit = erow == first
        sel = jnp.logical_or(sel, hit)
        m = jnp.where(hit, -jnp.inf, m)
    w = jnp.where(sel, s, 0.0)
    w = w / jnp.sum(w, axis=0, keepdims=True) * ROUTE_SCALE
    sel_f = jnp.where(sel, 1.0, 0.0)
    t_row = lax.broadcasted_iota(jnp.int32, (tm, tm), 0)
    t_col = lax.broadcasted_iota(jnp.int32, (tm, tm), 1)
    same_tile = (t_row // MOE_TILE) == (t_col // MOE_TILE)
    earlier = jnp.where(jnp.logical_and(t_row < t_col, same_tile), 1.0, 0.0).astype(BF16)
    slot = jnp.dot(sel_f.astype(BF16), earlier, preferred_element_type=F32)
    fits = jnp.logical_and(sel, slot < MOE_CAP)
    zpad = jnp.zeros((WC_LANES - N_EXPERTS, tm), F32)
    wc_ref[...] = jnp.concatenate([jnp.where(fits, w, 0.0), zpad], axis=0).T
    wo_ref[...] = jnp.concatenate([jnp.where(fits, 0.0, w), zpad], axis=0).T
    pos = jnp.where(fits, slot, -1.0)
    for k in range(tm // MOE_TILE):
        lanes = slice(k * MOE_TILE, (k + 1) * MOE_TILE)
        pos_ref[k] = pos[:, lanes]
        cnt_ref[k] = jnp.broadcast_to(jnp.sum(sel_f[:, lanes], axis=1, keepdims=True),
                                      (N_EXPERTS, WC_LANES))


def _router(x, w_router, bias):
    T, D = x.shape
    assert T % MOE_TILE == 0
    nt = T // MOE_TILE
    k = ROUTER_TILES if nt % ROUTER_TILES == 0 else 1
    tm = k * MOE_TILE
    return pl.pallas_call(
        _router_kernel,
        grid=(nt // k,),
        in_specs=[pl.BlockSpec((tm, D), lambda i: (i, 0)),
                  pl.BlockSpec((N_EXPERTS, D), lambda i: (0, 0)),
                  pl.BlockSpec((N_EXPERTS, 1), lambda i: (0, 0))],
        out_specs=[pl.BlockSpec((tm, WC_LANES), lambda i: (i, 0)),
                   pl.BlockSpec((tm, WC_LANES), lambda i: (i, 0)),
                   pl.BlockSpec((k, N_EXPERTS, MOE_TILE), lambda i: (i, 0, 0)),
                   pl.BlockSpec((k, N_EXPERTS, WC_LANES), lambda i: (i, 0, 0))],
        out_shape=[jax.ShapeDtypeStruct((T, WC_LANES), F32),
                   jax.ShapeDtypeStruct((T, WC_LANES), F32),
                   jax.ShapeDtypeStruct((nt, N_EXPERTS, MOE_TILE), F32),
                   jax.ShapeDtypeStruct((nt, N_EXPERTS, WC_LANES), F32)],
        compiler_params=_params("parallel"),
        name="router",
    )(x, w_router.T, bias.reshape(N_EXPERTS, 1))


def _slot_one_hot(pos_ref, e0, n, ncap, tm):
    slot = lax.broadcasted_iota(jnp.int32, (ncap, tm), 0).astype(F32)
    return jnp.concatenate([jnp.where(pos_ref[e:e + 1, :] == slot, 1.0, 0.0).astype(BF16)
                            for e in range(e0, e0 + n)], axis=0)


def _when_slots_used(used, body):
    @pl.when(used <= MOE_CAP_FAST)
    def _():
        body(MOE_CAP_FAST)

    @pl.when(used > MOE_CAP_FAST)
    def _():
        body(MOE_CAP)


def _dispatch_kernel(used_ref, xb_ref, wc_ref, pos_ref, xs_ref):
    tm = xb_ref.shape[0]
    wc = wc_ref[...]
    hi = wc.astype(BF16)
    lo = (wc - hi.astype(F32)).astype(BF16)
    x_aug = jnp.concatenate([xb_ref[...], hi[:, :N_EXPERTS], lo[:, :N_EXPERTS]], axis=1)

    def run(ncap):
        for g in range(N_EXPERTS // MOE_EGROUP):
            p = _slot_one_hot(pos_ref, g * MOE_EGROUP, MOE_EGROUP, ncap, tm)
            rows = jnp.dot(p, x_aug, preferred_element_type=F32).astype(BF16)
            for k in range(MOE_EGROUP):
                r0 = (g * MOE_EGROUP + k) * MOE_CAP
                xs_ref[r0:r0 + ncap, :] = rows[k * ncap:(k + 1) * ncap, :]
                if ncap < MOE_CAP:
                    xs_ref[r0 + ncap:r0 + MOE_CAP, :] = jnp.zeros((MOE_CAP - ncap, x_aug.shape[1]), BF16)

    _when_slots_used(used_ref[pl.program_id(0)], run)


def _dispatch(xb, wc, pos, used):
    T, D = xb.shape
    nt = T // MOE_TILE
    return pl.pallas_call(
        _dispatch_kernel,
        grid_spec=pltpu.PrefetchScalarGridSpec(
            num_scalar_prefetch=1,
            grid=(nt,),
            in_specs=[pl.BlockSpec((MOE_TILE, D), lambda i, u: (i, 0)),
                      pl.BlockSpec((MOE_TILE, WC_LANES), lambda i, u: (i, 0)),
                      pl.BlockSpec((None, N_EXPERTS, MOE_TILE), lambda i, u: (i, 0, 0))],
            out_specs=pl.BlockSpec((None, N_EXPERTS * MOE_CAP, D + WC_LANES), lambda i, u: (i, 0, 0))),
        out_shape=jax.ShapeDtypeStruct((nt, N_EXPERTS * MOE_CAP, D + WC_LANES), BF16),
        compiler_params=_params("parallel"),
        name="moe_dispatch",
    )(used, xb, wc, pos)


def _swiglu_rows(x, col, wg_ref, wu_ref, wd_ref):
    hg = jnp.dot(x, wg_ref[...].astype(BF16), preferred_element_type=F32)
    hu = jnp.dot(x, wu_ref[...].astype(BF16), preferred_element_type=F32)
    h = hg * _sigmoid(hg) * hu
    if col is not None:
        h = h * col
    return jnp.dot(h.astype(BF16), wd_ref[...].astype(BF16), preferred_element_type=F32)


def _experts_kernel(used_ref, xs_ref, wg32_ref, wu32_ref, wd32_ref, ys_ref, wg_ref, wu_ref, wd_ref):
    e = pl.program_id(0)
    j = pl.program_id(1)

    @pl.when(j == 0)
    def _():
        wg_ref[...] = wg32_ref[...].astype(BF16)
        wu_ref[...] = wu32_ref[...].astype(BF16)
        wd_ref[...] = wd32_ref[...].astype(BF16)

    tb, cap, width = xs_ref.shape
    d = width - WC_LANES

    def run(ncap):
        xa = xs_ref[:, 0:ncap, :].reshape(tb * ncap, width)
        aug = xa[:, d:].astype(F32)
        lane = lax.broadcasted_iota(jnp.int32, aug.shape, 1)
        mine = jnp.logical_or(lane == e, lane == e + N_EXPERTS)
        col = jnp.sum(jnp.where(mine, aug, 0.0), axis=1, keepdims=True)
        y = _swiglu_rows(xa[:, :d], col, wg_ref, wu_ref, wd_ref)
        ys_ref[:, 0:ncap, :] = y.astype(BF16).reshape(tb, ncap, d)
        if ncap < cap:
            ys_ref[:, ncap:cap, :] = jnp.zeros((tb, cap - ncap, d), BF16)

    _when_slots_used(used_ref[e * pl.num_programs(1) + j], run)


def _experts(xs, cnt, wg, wu, wd, layer):
    nt, _, width = xs.shape
    _, n_e, D, F = wg.shape
    tb = _pick_tile(nt, MOE_TILES_PER_STEP) if nt > MOE_TILES_PER_STEP else nt
    if nt % tb:
        tb = 1
    steps = nt // tb
    used = jnp.max(cnt.reshape(steps, tb, n_e), axis=1).T.reshape(n_e * steps).astype(jnp.int32)
    return pl.pallas_call(
        _experts_kernel,
        grid_spec=pltpu.PrefetchScalarGridSpec(
            num_scalar_prefetch=1,
            grid=(n_e, steps),
            in_specs=[pl.BlockSpec((tb, MOE_CAP, width), lambda e, j, u: (j, e, 0)),
                      pl.BlockSpec((None, None, D, F), lambda e, j, u: (layer, e, 0, 0)),
                      pl.BlockSpec((None, None, D, F), lambda e, j, u: (layer, e, 0, 0)),
                      pl.BlockSpec((None, None, F, D), lambda e, j, u: (layer, e, 0, 0))],
            out_specs=pl.BlockSpec((tb, MOE_CAP, D), lambda e, j, u: (j, e, 0)),
            scratch_shapes=[pltpu.VMEM((D, F), BF16), pltpu.VMEM((D, F), BF16),
                            pltpu.VMEM((F, D), BF16)]),
        out_shape=jax.ShapeDtypeStruct((nt, n_e * MOE_CAP, D), BF16),
        compiler_params=_params("parallel", "arbitrary"),
        name="moe_experts",
    )(used, xs, wg, wu, wd)


def _combine_kernel(used_ref, ys_ref, pos_ref, x_ref, xb_ref, yo_ref, sg_ref, su_ref, sd_ref, g_ref,
                    b_ref, y_ref, yb_ref, *, alpha, t0):
    tm = x_ref.shape[0]

    def run(ncap):
        acc = yo_ref[...] + _swiglu_rows(xb_ref[...], None, sg_ref, su_ref, sd_ref)
        for g in range(N_EXPERTS // MOE_EGROUP):
            p = _slot_one_hot(pos_ref, g * MOE_EGROUP, MOE_EGROUP, ncap, tm)
            ys = jnp.concatenate([ys_ref[e * MOE_CAP:e * MOE_CAP + ncap, :]
                                  for e in range(g * MOE_EGROUP, (g + 1) * MOE_EGROUP)], axis=0)
            acc = acc + lax.dot_general(p, ys, (((0,), (0,)), ((), ())), preferred_element_type=F32)
        y = _layer_norm(alpha * x_ref[...] + acc, g_ref[...], b_ref[...])
        y_ref[...] = y
        yb_ref[...] = y.astype(BF16)

    _when_slots_used(used_ref[t0 + pl.program_id(0)], run)


def _combine(ys, pos, used, x, xb, y_over, sg, su, sd, g, b, alpha, layer, row0, n):
    D = x.shape[1]
    F = sg.shape[2]
    assert row0 % MOE_TILE == 0 and n % MOE_TILE == 0
    t0 = row0 // MOE_TILE
    row_in = pl.BlockSpec((MOE_TILE, D), lambda i, u: (t0 + i, 0))
    row_out = pl.BlockSpec((MOE_TILE, D), lambda i, u: (i, 0))
    vec = pl.BlockSpec((1, D), lambda i, u: (0, 0))
    return pl.pallas_call(
        functools.partial(_combine_kernel, alpha=alpha, t0=t0),
        grid_spec=pltpu.PrefetchScalarGridSpec(
            num_scalar_prefetch=1,
            grid=(n // MOE_TILE,),
            in_specs=[pl.BlockSpec((None, N_EXPERTS * MOE_CAP, D), lambda i, u: (t0 + i, 0, 0)),
                      pl.BlockSpec((None, N_EXPERTS, MOE_TILE), lambda i, u: (t0 + i, 0, 0)),
                      row_in, row_in, row_in,
                      pl.BlockSpec((None, D, F), lambda i, u: (layer, 0, 0)),
                      pl.BlockSpec((None, D, F), lambda i, u: (layer, 0, 0)),
                      pl.BlockSpec((None, F, D), lambda i, u: (layer, 0, 0)),
                      vec, vec],
            out_specs=[row_out, row_out]),
        out_shape=[jax.ShapeDtypeStruct((n, D), F32), jax.ShapeDtypeStruct((n, D), BF16)],
        compiler_params=_params("parallel"),
        name="moe_combine",
    )(used, ys, pos, x, xb, y_over, sg, su, sd, g.reshape(1, D), b.reshape(1, D))


def _dense_experts_kernel(xb_ref, wo_ref, wg_ref, wu_ref, wd_ref, y_ref):
    e = pl.program_id(1)

    @pl.when(e == 0)
    def _():
        y_ref[...] = jnp.zeros_like(y_ref)

    lane = lax.broadcasted_iota(jnp.int32, wo_ref.shape, 1)
    col = jnp.sum(jnp.where(lane == e, wo_ref[...], 0.0), axis=1, keepdims=True)
    y_ref[...] += _swiglu_rows(xb_ref[...], col, wg_ref, wu_ref, wd_ref)


def _dense_experts(xb, wo, wg, wu, wd, layer):
    T, D = xb.shape
    _, n_e, _, F = wg.shape
    tm = _pick_tile(T, 1024)
    return pl.pallas_call(
        _dense_experts_kernel,
        grid=(T // tm, n_e),
        in_specs=[pl.BlockSpec((tm, D), lambda i, e: (i, 0)),
                  pl.BlockSpec((tm, WC_LANES), lambda i, e: (i, 0)),
                  pl.BlockSpec((None, None, D, F), lambda i, e: (layer, e, 0, 0)),
                  pl.BlockSpec((None, None, D, F), lambda i, e: (layer, e, 0, 0)),
                  pl.BlockSpec((None, None, F, D), lambda i, e: (layer, e, 0, 0))],
        out_specs=pl.BlockSpec((tm, D), lambda i, e: (i, 0)),
        out_shape=jax.ShapeDtypeStruct((T, D), F32),
        compiler_params=_params("parallel", "arbitrary"),
        name="moe_dense_overflow",
    )(xb, wo, wg, wu, wd)


def _moe(xb, x, w_router, bias, wg, wu, wd, sg, su, sd, g, b, alpha, layer, ranges):
    wc, wo, pos, cnt = _router(x, w_router, bias)
    cnt = cnt[:, :, 0]
    tile_used = jnp.max(cnt, axis=1).astype(jnp.int32)
    ys = _experts(_dispatch(xb, wc, pos, tile_used), cnt, wg, wu, wd, layer)
    y_over = lax.cond(jnp.max(cnt) > MOE_CAP,
                      lambda: _dense_experts(xb, wo, wg, wu, wd, layer),
                      lambda: jnp.zeros(x.shape, F32))
    return [_combine(ys, pos, tile_used, x, xb, y_over, sg, su, sd, g, b, alpha, layer, row0, n)
            for row0, n in ranges]


def kernel(x_prompt, x_sample, cache_sb_k, cache_sb_v, state_hgrn, ln_in_g, ln_in_b, w_in, w_out, hg_norm, hg_lb_logits, ln1_g, ln1_b, w_router, router_bias, w_exp_gate, w_exp_up, w_exp_down, w_sh_gate, w_sh_up, w_sh_down, ln2_g, ln2_b):
    B, S, D = x_prompt.shape
    Bs, Ts, _ = x_sample.shape
    depth = w_in.shape[0]
    n_p = B * S
    alpha = (2 * depth) ** 0.25

    x, xb = _entry_ln(x_prompt.reshape(n_p, D), x_sample.reshape(Bs * Ts, D), ln_in_g, ln_in_b)

    sp, sd = [], []
    kp = vp = kd = vd = None
    for l in range(depth):
        w_in_b = w_in[l].astype(BF16)
        proj = _in_proj(xb, w_in_b)
        kb_p, vb_p, kp, vp = _kv_proj(xb, w_in_b, kp, vp, l, depth, 0, n_p)
        kb_s, vb_s, kd, vd = _kv_proj(xb, w_in_b, kd, vd, l, depth, n_p, Bs * Ts)
        o_sb = _sb_prompt(proj, kb_p, vb_p, B, S)
        o_sb = _sb_sample(proj, kb_s, vb_s, cache_sb_k, cache_sb_v, o_sb, l, n_p, Bs, Ts)
        o_hg, s_p = _hgrn(proj, hg_lb_logits, hg_norm[l], None, None, l, 0, B, S)
        o_hg, s_s = _hgrn(proj, hg_lb_logits, hg_norm[l], state_hgrn[l], o_hg, l, n_p, Bs, Ts)
        x, xb = _merge_out(proj, o_sb, o_hg, x, w_out[l].astype(BF16), ln1_g[l], ln1_b[l], alpha)
        last = l == depth - 1
        ranges = [(0, n_p), (n_p, Bs * Ts)] if last else [(0, n_p + Bs * Ts)]
        outs = _moe(xb, x, w_router[l], router_bias[l], w_exp_gate, w_exp_up, w_exp_down,
                    w_sh_gate, w_sh_up, w_sh_down, ln2_g[l], ln2_b[l], alpha, l, ranges)
        if not last:
            x, xb = outs[0]
        sp.append(s_p)
        sd.append(s_s)
    kv_p = (depth, B, S, N_HEADS, HEAD_DIM)
    kv_s = (depth, Bs, Ts, N_HEADS, HEAD_DIM)
    return (outs[0][0].reshape(B, S, D), outs[1][0].reshape(Bs, Ts, D),
            kp.reshape(kv_p), vp.reshape(kv_p), jnp.stack(sp),
            kd.reshape(kv_s), vd.reshape(kv_s), jnp.stack(sd))
```

```python
import functools
import math

import jax
import jax.numpy as jnp
from jax import lax
from jax.experimental import pallas as pl
from jax.experimental.pallas import tpu as pltpu

F32 = jnp.float32
BF16 = jnp.bfloat16

N_HEADS = 8
HEAD_DIM = 128
N_EXPERTS = 64
TOP_K = 8
N_GROUPS = 8
TOPK_GROUPS = 4
GROUP_SIZE = N_EXPERTS // N_GROUPS
ROUTE_SCALE = 2.5
NEG_BIG = -1e9
LB_FLOOR = 1e-30
LN_EPS = 1e-5
RMS_EPS = 1e-6
LOG2E = 1.4426950408889634
N_SPLITS = 9
P_Q, P_HQ, P_HF, P_HI, P_HG, P_GA, P_GB = range(7)

LANES = 128
SUBLANES = 8
VMEM_LIMIT = 48 * 1024 * 1024

HG_CHUNK = 128
HG_SUB = 32
SB_BLOCK = 256
SB_GROUP = 8
WC_LANES = 128
MOE_TILE = 256
ROUTER_TILES = 2
MOE_CAP = 64
MOE_CAP_FAST = 48
MOE_EGROUP = 16
MOE_TILES_PER_STEP = 24


def _pick_tile(n, pref):
    if n <= pref:
        return n
    for t in range(pref, 7, -1):
        if n % t == 0 and t % SUBLANES == 0:
            return t
    raise ValueError(f"no tile for {n}")


def _params(*sem):
    return pltpu.CompilerParams(dimension_semantics=sem, vmem_limit_bytes=VMEM_LIMIT)


def _layer_norm(x, g, b):
    mu = jnp.mean(x, axis=-1, keepdims=True)
    xc = x - mu
    var = jnp.mean(xc * xc, axis=-1, keepdims=True)
    return xc * lax.rsqrt(var + LN_EPS) * g + b


def _sigmoid(x):
    return 1.0 / (1.0 + jnp.exp(-x))


def _ln_kernel(xp_ref, xs_ref, g_ref, b_ref, y_ref, yb_ref, *, p_tiles):
    x = jnp.where(pl.program_id(0) < p_tiles, xp_ref[...], xs_ref[...])
    y = _layer_norm(x, g_ref[...], b_ref[...])
    y_ref[...] = y
    yb_ref[...] = y.astype(BF16)


def _entry_ln(xp, xs, g, b):
    n_p, D = xp.shape
    n_s = xs.shape[0]
    tm = _pick_tile(math.gcd(n_p, n_s), 512)
    p_tiles = n_p // tm
    T = n_p + n_s
    return pl.pallas_call(
        functools.partial(_ln_kernel, p_tiles=p_tiles),
        grid=(T // tm,),
        in_specs=[pl.BlockSpec((tm, D), lambda i: (jnp.minimum(i, p_tiles - 1), 0)),
                  pl.BlockSpec((tm, D), lambda i: (jnp.maximum(i - p_tiles, 0), 0)),
                  pl.BlockSpec((1, D), lambda i: (0, 0)),
                  pl.BlockSpec((1, D), lambda i: (0, 0))],
        out_specs=[pl.BlockSpec((tm, D), lambda i: (i, 0)),
                   pl.BlockSpec((tm, D), lambda i: (i, 0))],
        out_shape=[jax.ShapeDtypeStruct((T, D), F32), jax.ShapeDtypeStruct((T, D), BF16)],
        compiler_params=_params("parallel"),
        name="entry_ln",
    )(xp, xs, g.reshape(1, D), b.reshape(1, D))


def _in_proj_kernel(x_ref, w_ref, o_ref):
    o_ref[...] = jnp.dot(x_ref[...], w_ref[...], preferred_element_type=F32)


def _in_proj(xb, w_in_b):
    T, D = xb.shape
    tm = _pick_tile(T, 1024)
    return pl.pallas_call(
        _in_proj_kernel,
        grid=(N_SPLITS - 2, T // tm),
        in_specs=[pl.BlockSpec((tm, D), lambda j, i: (i, 0)),
                  pl.BlockSpec((D, D), lambda j, i: (0, jnp.where(j >= 1, j + 2, j)))],
        out_specs=pl.BlockSpec((None, tm, D), lambda j, i: (j, i, 0)),
        out_shape=jax.ShapeDtypeStruct((N_SPLITS - 2, T, D), F32),
        compiler_params=_params("parallel", "parallel"),
        name="in_proj",
    )(xb, w_in_b)


def _kv_proj_kernel(x_ref, wk_ref, wv_ref, *refs):
    kb_ref, vb_ref, ko_ref, vo_ref = refs[-4:]
    tm = x_ref.shape[0]
    x = x_ref[...]
    for w_ref, b_ref, o_ref in ((wk_ref, kb_ref, ko_ref), (wv_ref, vb_ref, vo_ref)):
        y = jnp.dot(x, w_ref[...], preferred_element_type=F32)
        b_ref[...] = y.astype(BF16)
        for h in range(N_HEADS):
            o_ref[pl.ds(h, tm, stride=N_HEADS), :] = y[:, h * HEAD_DIM:(h + 1) * HEAD_DIM]


def _kv_proj(xb, w_in_b, k_prev, v_prev, layer, depth, row0, n):
    D = xb.shape[1]
    tm = _pick_tile(n, 512)
    assert row0 % tm == 0
    rb = row0 // tm
    out_spec = pl.BlockSpec((None, tm * N_HEADS, HEAD_DIM), lambda i: (layer, i, 0))
    out_sds = jax.ShapeDtypeStruct((depth, n * N_HEADS, HEAD_DIM), F32)
    args = [xb, w_in_b, w_in_b]
    in_specs = [pl.BlockSpec((tm, D), lambda i: (rb + i, 0)),
                pl.BlockSpec((D, D), lambda i: (0, 1)),
                pl.BlockSpec((D, D), lambda i: (0, 2))]
    aliases = {}
    if k_prev is not None:
        args += [k_prev, v_prev]
        in_specs += [pl.BlockSpec(memory_space=pl.ANY)] * 2
        aliases = {3: 2, 4: 3}
    return pl.pallas_call(
        _kv_proj_kernel,
        grid=(n // tm,),
        in_specs=in_specs,
        out_specs=[pl.BlockSpec((tm, D), lambda i: (i, 0)), pl.BlockSpec((tm, D), lambda i: (i, 0)),
                   out_spec, out_spec],
        out_shape=[jax.ShapeDtypeStruct((n, D), BF16), jax.ShapeDtypeStruct((n, D), BF16),
                   out_sds, out_sds],
        input_output_aliases=aliases,
        compiler_params=_params("parallel"),
        name="kv_proj",
    )(*args)


def _neg_tri(tk):
    r = lax.broadcasted_iota(jnp.int32, (tk, tk), 0)
    c = lax.broadcasted_iota(jnp.int32, (tk, tk), 1)
    return jnp.where(r >= c, -1.0, 0.0).astype(BF16)


def _strict_lower(tq, tk):
    return (lax.broadcasted_iota(jnp.int32, (tq, tk), 1)
            < lax.broadcasted_iota(jnp.int32, (tq, tk), 0))


def _sb_z(qs, kb):
    return lax.dot_general(qs, kb, (((1,), (1,)), ((), ())), preferred_element_type=F32)


def _sb_cumsum_lhs(z, tk, mask):
    sp = jnp.maximum(z, 0.0) + jnp.log(1.0 + jnp.exp2(jnp.abs(z) * (-LOG2E)))
    if mask is not None:
        sp = jnp.where(mask, sp, 0.0)
    sp = sp.astype(BF16)
    n = z.shape[1] // tk
    return sp if n == 1 else jnp.concatenate([sp[:, a * tk:(a + 1) * tk] for a in range(n)], axis=0)


def _sb_weights(z, c, carry, tk, mask):
    tq = z.shape[0]
    n = z.shape[1] // tk
    ws = [None] * n
    for a in range(n - 1, -1, -1):
        ca = c[a * tq:(a + 1) * tq, :]
        w = jnp.exp2((z[:, a * tk:(a + 1) * tk] + ca + carry) * LOG2E)
        if mask is not None:
            w = jnp.where(mask, w, 0.0)
        ws[a] = w.astype(BF16)
        carry = carry + ca[:, 0:1]
    return (ws[0] if n == 1 else jnp.concatenate(ws, axis=1)), carry


def _sb_prompt_kernel(q_ref, kb_scr, vb_scr, o_ref, tri_scr, *, blk):
    i = pl.program_id(2)

    @pl.when(i == 0)
    def _():
        tri_scr[...] = _neg_tri(blk)

    qs = (q_ref[...] * (1.0 / math.sqrt(HEAD_DIM))).astype(BF16)

    def block_rows(j):
        return pl.ds(pl.multiple_of(j * blk, blk), blk)

    def sweep(j_lo, n, co, diag_last, zs=None):
        carry, o = co
        rows = [block_rows(j_lo + a) for a in range(n)]
        masks = [_strict_lower(blk, blk) if diag_last and a == n - 1 else None for a in range(n)]
        if zs is None:
            zs = [_sb_z(qs, kb_scr[r, :]) for r in rows]
        cs = [jnp.dot(_sb_cumsum_lhs(z, blk, m), tri_scr[...], preferred_element_type=F32)
              for z, m in zip(zs, masks)]
        ws = []
        for a in range(n - 1, -1, -1):
            w, carry = _sb_weights(zs[a], cs[a], carry, blk, masks[a])
            ws.append(w)
        if n == 1:
            return carry, o + jnp.dot(ws[0], vb_scr[rows[0], :], preferred_element_type=F32)
        v_desc = jnp.concatenate([vb_scr[rows[a], :] for a in range(n - 1, -1, -1)], axis=0)
        return carry, o + jnp.dot(jnp.concatenate(ws, axis=1), v_desc, preferred_element_type=F32)

    co = (jnp.zeros((blk, 1), F32), jnp.zeros((blk, HEAD_DIM), F32))
    rem = i % SB_GROUP
    co = lax.switch(rem, [functools.partial(lambda r, co: sweep(i - r, r + 1, co, True), r)
                          for r in range(SB_GROUP)], co)
    co = lax.fori_loop(0, i // SB_GROUP,
                       lambda p, co: sweep(i - rem - SB_GROUP * (p + 1), SB_GROUP, co, False), co)
    o_ref[...] = co[1]


def _sb_prompt(proj, kb, vb, B, S):
    D = N_HEADS * HEAD_DIM
    t_all = proj.shape[1]
    blk = _pick_tile(S, SB_BLOCK)
    nq = S // blk
    return pl.pallas_call(
        functools.partial(_sb_prompt_kernel, blk=blk),
        grid=(B, N_HEADS, nq),
        in_specs=[pl.BlockSpec((None, blk, HEAD_DIM), lambda b, h, i: (P_Q, b * nq + i, h)),
                  pl.BlockSpec((S, HEAD_DIM), lambda b, h, i: (b, h)),
                  pl.BlockSpec((S, HEAD_DIM), lambda b, h, i: (b, h))],
        out_specs=pl.BlockSpec((blk, HEAD_DIM), lambda b, h, i: (b * nq + i, h)),
        out_shape=jax.ShapeDtypeStruct((t_all, D), F32),
        scratch_shapes=[pltpu.VMEM((blk, blk), BF16)],
        compiler_params=_params("parallel", "parallel", "arbitrary"),
        name="sb_prompt",
    )(proj, kb, vb)


def _sb_sample_kernel(q_ref, k_ref, v_ref, pk_ref, pv_ref, o_in_ref, o_ref, *, pblk):
    del o_in_ref
    T = q_ref.shape[0]
    P = pk_ref.shape[0] // N_HEADS

    def head_rows(ref, h):
        return ref[pl.ds(h, P, stride=N_HEADS), :].astype(BF16)

    heads = range(N_HEADS)
    cols = [slice(h * HEAD_DIM, (h + 1) * HEAD_DIM) for h in heads]
    mask = _strict_lower(T, T)
    qs = [(q_ref[:, cols[h]] * (1.0 / math.sqrt(HEAD_DIM))).astype(BF16) for h in heads]
    z_new = [_sb_z(qs[h], k_ref[:, cols[h]]) for h in heads]
    z_past = [_sb_z(qs[h], head_rows(pk_ref, h)) for h in heads]
    lhs_new = jnp.concatenate([_sb_cumsum_lhs(z_new[h], T, mask) for h in heads], axis=0)
    lhs_past = jnp.concatenate([_sb_cumsum_lhs(z_past[h], pblk, None) for h in heads], axis=0)
    c_new = jnp.dot(lhs_new, _neg_tri(T), preferred_element_type=F32)
    c_past = jnp.dot(lhs_past, _neg_tri(pblk), preferred_element_type=F32)
    rows_past = (P // pblk) * T
    for h in heads:
        w_new, carry = _sb_weights(z_new[h], c_new[h * T:(h + 1) * T], jnp.zeros((T, 1), F32), T, mask)
        w_past, _ = _sb_weights(z_past[h], c_past[h * rows_past:(h + 1) * rows_past], carry, pblk, None)
        o_ref[:, cols[h]] = (jnp.dot(w_new, v_ref[:, cols[h]], preferred_element_type=F32)
                             + jnp.dot(w_past, head_rows(pv_ref, h), preferred_element_type=F32))


def _sb_sample(proj, kb, vb, cache_k, cache_v, o_sb, layer, row0, B, T):
    D = N_HEADS * HEAD_DIM
    depth, _, P = cache_k.shape[:3]
    pblk = _pick_tile(P, SB_BLOCK)
    rb = row0 // T
    cache_k = cache_k.reshape(depth, B, P * N_HEADS, HEAD_DIM)
    cache_v = cache_v.reshape(depth, B, P * N_HEADS, HEAD_DIM)
    cache_spec = pl.BlockSpec((None, None, P * N_HEADS, HEAD_DIM), lambda b: (layer, b, 0, 0))
    return pl.pallas_call(
        functools.partial(_sb_sample_kernel, pblk=pblk),
        grid=(B,),
        in_specs=[pl.BlockSpec((None, T, D), lambda b: (P_Q, rb + b, 0)),
                  pl.BlockSpec((T, D), lambda b: (b, 0)),
                  pl.BlockSpec((T, D), lambda b: (b, 0)),
                  cache_spec, cache_spec,
                  pl.BlockSpec(memory_space=pl.ANY)],
        out_specs=pl.BlockSpec((T, D), lambda b: (rb + b, 0)),
        out_shape=jax.ShapeDtypeStruct(o_sb.shape, F32),
        input_output_aliases={5: 0},
        compiler_params=_params("parallel"),
        name="sb_sample",
    )(proj, kb, vb, cache_k, cache_v, o_sb)


def _cumsum_rows(x):
    n = x.shape[0]
    h1 = x.astype(BF16)
    r1 = x - h1.astype(F32)
    h2 = r1.astype(BF16)
    h3 = (r1 - h2.astype(F32)).astype(BF16)
    row = lax.broadcasted_iota(jnp.int32, (n, 3 * n), 0)
    col = lax.broadcasted_iota(jnp.int32, (n, 3 * n), 1)
    col = jnp.where(col >= 2 * n, col - 2 * n, jnp.where(col >= n, col - n, col))
    tril3 = jnp.where(row >= col, 1.0, 0.0).astype(BF16)
    return jnp.dot(tril3, jnp.concatenate([h1, h2, h3], axis=0), preferred_element_type=F32)


def _hgrn_kernel(*refs, layer, has_s0):
    if has_s0:
        q_ref, f_ref, i_ref, g_ref, lbl_ref, norm_ref, s0_ref, _, o_ref, sf_ref, st_scr = refs
    else:
        q_ref, f_ref, i_ref, g_ref, lbl_ref, norm_ref, o_ref, sf_ref, st_scr = refs
    c = pl.program_id(1)
    C = q_ref.shape[0]
    n_sub = C // HG_SUB

    @pl.when(c == 0)
    def _():
        for h in range(N_HEADS):
            if has_s0:
                st_scr[h] = s0_ref[h].T
            else:
                st_scr[h] = jnp.zeros((HEAD_DIM, HEAD_DIM), F32)

    lg = lbl_ref[...]
    ex = jnp.exp(lg - jnp.max(lg, axis=0, keepdims=True))
    soft = ex / jnp.sum(ex, axis=0, keepdims=True)
    lb = jnp.zeros((1, lg.shape[1]), F32)
    for m in range(1, layer + 1):
        lb = lb + soft[m:m + 1, :]

    a = f_ref[...]
    u = jnp.exp(-jnp.abs(a))
    lbp = jnp.maximum(lb, LB_FLOOR)
    num = jnp.where(a >= 0.0, 1.0 + lbp * u, u + lbp)
    logf = jnp.minimum(jnp.log(num) - jnp.log(1.0 + u), 0.0)
    kk_all = 1.0 - jnp.exp(logf)
    b_all = _cumsum_rows(logf)
    qraw = q_ref[...]
    q_all = qraw * _sigmoid(qraw)
    graw = g_ref[...]
    gate_all = graw * _sigmoid(graw)
    v_all = i_ref[...]

    heads = range(N_HEADS)
    cols = [slice(h * HEAD_DIM, (h + 1) * HEAD_DIM) for h in heads]
    o_inter, scores, vbs = [], [], []
    for h in heads:
        b = b_all[:, cols[h]]
        q = q_all[:, cols[h]]
        kk = kk_all[:, cols[h]]
        vb = v_all[:, cols[h]].astype(BF16)
        st = st_scr[h]
        qt = (q * jnp.exp(b)).astype(BF16)
        o_inter.append(lax.dot_general(qt, st.astype(BF16), (((1,), (1,)), ((), ())),
                                       preferred_element_type=F32))
        sc_h = []
        for i in range(n_sub):
            lo_r, hi_r = i * HG_SUB, (i + 1) * HG_SUB
            if i == 0:
                ref_row = jnp.zeros((1, HEAD_DIM), F32)
            else:
                ref_row = b[lo_r - 1:lo_r, :]
            qh = (q[lo_r:hi_r] * jnp.exp(b[lo_r:hi_r] - ref_row)).astype(BF16)
            kh = (kk[0:hi_r] * jnp.exp(ref_row - b[0:hi_r])).astype(BF16)
            sc_h.append(lax.dot_general(qh, kh, (((1,), (1,)), ((), ())),
                                        preferred_element_type=F32))
        b_last = b[C - 1:C, :]
        ke = (kk * jnp.exp(b_last - b)).astype(BF16)
        upd = lax.dot_general(vb, ke, (((0,), (0,)), ((), ())), preferred_element_type=F32)
        st_scr[h] = st * jnp.exp(b_last) + upd
        scores.append(sc_h)
        vbs.append(vb)
    for h in heads:
        parts = []
        for i in range(n_sub):
            lo_r, hi_r = i * HG_SUB, (i + 1) * HG_SUB
            tpos = lax.broadcasted_iota(jnp.int32, (HG_SUB, hi_r), 0) + lo_r
            spos = lax.broadcasted_iota(jnp.int32, (HG_SUB, hi_r), 1)
            sc = jnp.where(spos <= tpos, scores[h][i], 0.0).astype(BF16)
            parts.append(jnp.dot(sc, vbs[h][0:hi_r], preferred_element_type=F32))
        o = o_inter[h] + jnp.concatenate(parts, axis=0)
        ms = jnp.mean(o * o, axis=-1, keepdims=True)
        o_ref[:, cols[h]] = o * lax.rsqrt(ms + RMS_EPS) * norm_ref[...] * gate_all[:, cols[h]]

    @pl.when(c == pl.num_programs(1) - 1)
    def _():
        for h in range(N_HEADS):
            sf_ref[h] = st_scr[h].T


def _hgrn(proj, lb_logits, hg_norm, s0, o_prev, layer, row0, B, T):
    D = N_HEADS * HEAD_DIM
    C = min(HG_CHUNK, T)
    assert T % C == 0 and row0 % C == 0 and C % HG_SUB == 0
    nc = T // C
    rb = row0 // C
    depth = lb_logits.shape[0]
    t_all = proj.shape[1]

    def act_spec(split):
        return pl.BlockSpec((None, C, D), lambda b, c: (split, rb + b * nc + c, 0))

    in_specs = [act_spec(P_HQ), act_spec(P_HF), act_spec(P_HI), act_spec(P_HG),
                pl.BlockSpec((depth, D), lambda b, c: (0, 0)),
                pl.BlockSpec((1, HEAD_DIM), lambda b, c: (0, 0))]
    args = [proj, proj, proj, proj, lb_logits, hg_norm.reshape(1, HEAD_DIM)]
    if s0 is not None:
        in_specs.append(pl.BlockSpec((None, N_HEADS, HEAD_DIM, HEAD_DIM), lambda b, c: (b, 0, 0, 0)))
        in_specs.append(pl.BlockSpec(memory_space=pl.ANY))
        args += [s0, o_prev]
    return pl.pallas_call(
        functools.partial(_hgrn_kernel, layer=layer, has_s0=s0 is not None),
        grid=(B, nc),
        in_specs=in_specs,
        out_specs=[pl.BlockSpec((C, D), lambda b, c: (rb + b * nc + c, 0)),
                   pl.BlockSpec((None, N_HEADS, HEAD_DIM, HEAD_DIM), lambda b, c: (b, 0, 0, 0))],
        out_shape=[jax.ShapeDtypeStruct((t_all, D), F32),
                   jax.ShapeDtypeStruct((B, N_HEADS, HEAD_DIM, HEAD_DIM), F32)],
        input_output_aliases={} if s0 is None else {len(args) - 1: 0},
        scratch_shapes=[pltpu.VMEM((N_HEADS, HEAD_DIM, HEAD_DIM), F32)],
        compiler_params=_params("parallel", "arbitrary"),
        name="hgrn2",
    )(*args)


def _merge_out_kernel(ga_ref, gb_ref, oa_ref, ob_ref, x_ref, w_ref, g_ref, b_ref, y_ref, yb_ref, *, alpha):
    merged = _sigmoid(ga_ref[...]) * oa_ref[...] + _sigmoid(gb_ref[...]) * ob_ref[...]
    r = alpha * x_ref[...] + jnp.dot(merged.astype(BF16), w_ref[...], preferred_element_type=F32)
    y = _layer_norm(r, g_ref[...], b_ref[...])
    y_ref[...] = y
    yb_ref[...] = y.astype(BF16)


def _merge_out(proj, o_sb, o_hg, x, w_out_b, g, b, alpha):
    T, D = x.shape
    tm = _pick_tile(T, 256)
    row = pl.BlockSpec((tm, D), lambda i: (i, 0))
    vec = pl.BlockSpec((1, D), lambda i: (0, 0))
    return pl.pallas_call(
        functools.partial(_merge_out_kernel, alpha=alpha),
        grid=(T // tm,),
        in_specs=[pl.BlockSpec((None, tm, D), lambda i: (P_GA, i, 0)),
                  pl.BlockSpec((None, tm, D), lambda i: (P_GB, i, 0)),
                  row, row, row,
                  pl.BlockSpec((D, D), lambda i: (0, 0)), vec, vec],
        out_specs=[row, row],
        out_shape=[jax.ShapeDtypeStruct((T, D), F32), jax.ShapeDtypeStruct((T, D), BF16)],
        compiler_params=_params("parallel"),
        name="merge_out",
    )(proj, proj, o_sb, o_hg, x, w_out_b, g.reshape(1, D), b.reshape(1, D))


def _router_kernel(x_ref, wr_ref, bias_ref, wc_ref, wo_ref, pos_ref, cnt_ref):
    tm = x_ref.shape[0]
    logits = lax.dot_general(wr_ref[...], x_ref[...], (((1,), (1,)), ((), ())),
                             precision=lax.Precision.HIGHEST, preferred_element_type=F32)
    s = _sigmoid(logits)
    sb = s + bias_ref[...]
    sub = lax.broadcasted_iota(jnp.int32, (GROUP_SIZE, tm), 0)
    gs = []
    for g in range(N_GROUPS):
        blk = sb[g * GROUP_SIZE:(g + 1) * GROUP_SIZE, :]
        m1 = jnp.max(blk, axis=0, keepdims=True)
        first = jnp.min(jnp.where(blk == m1, sub, GROUP_SIZE), axis=0, keepdims=True)
        m2 = jnp.max(jnp.where(sub == first, -jnp.inf, blk), axis=0, keepdims=True)
        gs.append(m1 + m2)
    masked = []
    for g in range(N_GROUPS):
        beat = jnp.zeros((1, tm), jnp.int32)
        for g2 in range(N_GROUPS):
            if g2 == g:
                continue
            if g2 < g:
                beat = beat + (gs[g2] >= gs[g]).astype(jnp.int32)
            else:
                beat = beat + (gs[g2] > gs[g]).astype(jnp.int32)
        keep = beat < TOPK_GROUPS
        masked.append(jnp.where(keep, sb[g * GROUP_SIZE:(g + 1) * GROUP_SIZE, :], NEG_BIG))
    m = jnp.concatenate(masked, axis=0)
    erow = lax.broadcasted_iota(jnp.int32, (N_EXPERTS, tm), 0)
    sel = jnp.zeros((N_EXPERTS, tm), jnp.bool_)
    for _ in range(TOP_K):
        mx = jnp.max(m, axis=0, keepdims=True)
        first = jnp.min(jnp.where(m == mx, erow, N_EXPERTS), axis=0, keepdims=True)
        hit = erow == first
        sel = jnp.logical_or(sel, hit)
        m = jnp.where(hit, -jnp.inf, m)
    w = jnp.where(sel, s, 0.0)
    w = w / jnp.sum(w, axis=0, keepdims=True) * ROUTE_SCALE
    sel_f = jnp.where(sel, 1.0, 0.0)
    t_row = lax.broadcasted_iota(jnp.int32, (tm, tm), 0)
    t_col = lax.broadcasted_iota(jnp.int32, (tm, tm), 1)
    same_tile = (t_row // MOE_TILE) == (t_col // MOE_TILE)
    earlier = jnp.where(jnp.logical_and(t_row < t_col, same_tile), 1.0, 0.0).astype(BF16)
    slot = jnp.dot(sel_f.astype(BF16), earlier, preferred_element_type=F32)
    fits = jnp.logical_and(sel, slot < MOE_CAP)
    zpad = jnp.zeros((WC_LANES - N_EXPERTS, tm), F32)
    wc_ref[...] = jnp.concatenate([jnp.where(fits, w, 0.0), zpad], axis=0).T
    wo_ref[...] = jnp.concatenate([jnp.where(fits, 0.0, w), zpad], axis=0).T
    pos = jnp.where(fits, slot, -1.0)
    for k in range(tm // MOE_TILE):
        lanes = slice(k * MOE_TILE, (k + 1) * MOE_TILE)
        pos_ref[k] = pos[:, lanes]
        cnt_ref[k] = jnp.broadcast_to(jnp.sum(sel_f[:, lanes], axis=1, keepdims=True),
                                      (N_EXPERTS, WC_LANES))


def _router(x, w_router, bias):
    T, D = x.shape
    assert T % MOE_TILE == 0
    nt = T // MOE_TILE
    k = ROUTER_TILES if nt % ROUTER_TILES == 0 else 1
    tm = k * MOE_TILE
    return pl.pallas_call(
        _router_kernel,
        grid=(nt // k,),
        in_specs=[pl.BlockSpec((tm, D), lambda i: (i, 0)),
                  pl.BlockSpec((N_EXPERTS, D), lambda i: (0, 0)),
                  pl.BlockSpec((N_EXPERTS, 1), lambda i: (0, 0))],
        out_specs=[pl.BlockSpec((tm, WC_LANES), lambda i: (i, 0)),
                   pl.BlockSpec((tm, WC_LANES), lambda i: (i, 0)),
                   pl.BlockSpec((k, N_EXPERTS, MOE_TILE), lambda i: (i, 0, 0)),
                   pl.BlockSpec((k, N_EXPERTS, WC_LANES), lambda i: (i, 0, 0))],
        out_shape=[jax.ShapeDtypeStruct((T, WC_LANES), F32),
                   jax.ShapeDtypeStruct((T, WC_LANES), F32),
                   jax.ShapeDtypeStruct((nt, N_EXPERTS, MOE_TILE), F32),
                   jax.ShapeDtypeStruct((nt, N_EXPERTS, WC_LANES), F32)],
        compiler_params=_params("parallel"),
        name="router",
    )(x, w_router.T, bias.reshape(N_EXPERTS, 1))


def _slot_one_hot(pos_ref, e0, n, ncap, tm):
    slot = lax.broadcasted_iota(jnp.int32, (ncap, tm), 0).astype(F32)
    return jnp.concatenate([jnp.where(pos_ref[e:e + 1, :] == slot, 1.0, 0.0).astype(BF16)
                            for e in range(e0, e0 + n)], axis=0)


def _when_slots_used(used, body):
    @pl.when(used <= MOE_CAP_FAST)
    def _():
        body(MOE_CAP_FAST)

    @pl.when(used > MOE_CAP_FAST)
    def _():
        body(MOE_CAP)


def _dispatch_kernel(used_ref, xb_ref, wc_ref, pos_ref, xs_ref):
    tm = xb_ref.shape[0]
    wc = wc_ref[...]
    hi = wc.astype(BF16)
    lo = (wc - hi.astype(F32)).astype(BF16)
    x_aug = jnp.concatenate([xb_ref[...], hi[:, :N_EXPERTS], lo[:, :N_EXPERTS]], axis=1)

    def run(ncap):
        for g in range(N_EXPERTS // MOE_EGROUP):
            p = _slot_one_hot(pos_ref, g * MOE_EGROUP, MOE_EGROUP, ncap, tm)
            rows = jnp.dot(p, x_aug, preferred_element_type=F32).astype(BF16)
            for k in range(MOE_EGROUP):
                r0 = (g * MOE_EGROUP + k) * MOE_CAP
                xs_ref[r0:r0 + ncap, :] = rows[k * ncap:(k + 1) * ncap, :]
                if ncap < MOE_CAP:
                    xs_ref[r0 + ncap:r0 + MOE_CAP, :] = jnp.zeros((MOE_CAP - ncap, x_aug.shape[1]), BF16)

    _when_slots_used(used_ref[pl.program_id(0)], run)


def _dispatch(xb, wc, pos, used):
    T, D = xb.shape
    nt = T // MOE_TILE
    return pl.pallas_call(
        _dispatch_kernel,
        grid_spec=pltpu.PrefetchScalarGridSpec(
            num_scalar_prefetch=1,
            grid=(nt,),
            in_specs=[pl.BlockSpec((MOE_TILE, D), lambda i, u: (i, 0)),
                      pl.BlockSpec((MOE_TILE, WC_LANES), lambda i, u: (i, 0)),
                      pl.BlockSpec((None, N_EXPERTS, MOE_TILE), lambda i, u: (i, 0, 0))],
            out_specs=pl.BlockSpec((None, N_EXPERTS * MOE_CAP, D + WC_LANES), lambda i, u: (i, 0, 0))),
        out_shape=jax.ShapeDtypeStruct((nt, N_EXPERTS * MOE_CAP, D + WC_LANES), BF16),
        compiler_params=_params("parallel"),
        name="moe_dispatch",
    )(used, xb, wc, pos)


def _swiglu_rows(x, col, wg_ref, wu_ref, wd_ref):
    hg = jnp.dot(x, wg_ref[...].astype(BF16), preferred_element_type=F32)
    hu = jnp.dot(x, wu_ref[...].astype(BF16), preferred_element_type=F32)
    h = hg * _sigmoid(hg) * hu
    if col is not None:
        h = h * col
    return jnp.dot(h.astype(BF16), wd_ref[...].astype(BF16), preferred_element_type=F32)


def _experts_kernel(used_ref, xs_ref, wg32_ref, wu32_ref, wd32_ref, ys_ref, wg_ref, wu_ref, wd_ref):
    e = pl.program_id(0)
    j = pl.program_id(1)

    @pl.when(j == 0)
    def _():
        wg_ref[...] = wg32_ref[...].astype(BF16)
        wu_ref[...] = wu32_ref[...].astype(BF16)
        wd_ref[...] = wd32_ref[...].astype(BF16)

    tb, cap, width = xs_ref.shape
    d = width - WC_LANES

    def run(ncap):
        xa = xs_ref[:, 0:ncap, :].reshape(tb * ncap, width)
        aug = xa[:, d:].astype(F32)
        lane = lax.broadcasted_iota(jnp.int32, aug.shape, 1)
        mine = jnp.logical_or(lane == e, lane == e + N_EXPERTS)
        col = jnp.sum(jnp.where(mine, aug, 0.0), axis=1, keepdims=True)
        y = _swiglu_rows(xa[:, :d], col, wg_ref, wu_ref, wd_ref)
        ys_ref[:, 0:ncap, :] = y.astype(BF16).reshape(tb, ncap, d)
        if ncap < cap:
            ys_ref[:, ncap:cap, :] = jnp.zeros((tb, cap - ncap, d), BF16)

    _when_slots_used(used_ref[e * pl.num_programs(1) + j], run)


def _experts(xs, cnt, wg, wu, wd, layer):
    nt, _, width = xs.shape
    _, n_e, D, F = wg.shape
    tb = _pick_tile(nt, MOE_TILES_PER_STEP) if nt > MOE_TILES_PER_STEP else nt
    if nt % tb:
        tb = 1
    steps = nt // tb
    used = jnp.max(cnt.reshape(steps, tb, n_e), axis=1).T.reshape(n_e * steps).astype(jnp.int32)
    return pl.pallas_call(
        _experts_kernel,
        grid_spec=pltpu.PrefetchScalarGridSpec(
            num_scalar_prefetch=1,
            grid=(n_e, steps),
            in_specs=[pl.BlockSpec((tb, MOE_CAP, width), lambda e, j, u: (j, e, 0)),
                      pl.BlockSpec((None, None, D, F), lambda e, j, u: (layer, e, 0, 0)),
                      pl.BlockSpec((None, None, D, F), lambda e, j, u: (layer, e, 0, 0)),
                      pl.BlockSpec((None, None, F, D), lambda e, j, u: (layer, e, 0, 0))],
            out_specs=pl.BlockSpec((tb, MOE_CAP, D), lambda e, j, u: (j, e, 0)),
            scratch_shapes=[pltpu.VMEM((D, F), BF16), pltpu.VMEM((D, F), BF16),
                            pltpu.VMEM((F, D), BF16)]),
        out_shape=jax.ShapeDtypeStruct((nt, n_e * MOE_CAP, D), BF16),
        compiler_params=_params("parallel", "arbitrary"),
        name="moe_experts",
    )(used, xs, wg, wu, wd)


def _combine_kernel(used_ref, ys_ref, pos_ref, x_ref, xb_ref, *refs, alpha, t0, has_over):
    if has_over:
        yo_ref, sg_ref, su_ref, sd_ref, g_ref, b_ref, y_ref, yb_ref = refs
    else:
        sg_ref, su_ref, sd_ref, g_ref, b_ref, y_ref, yb_ref = refs
    tm = x_ref.shape[0]

    def run(ncap):
        acc = _swiglu_rows(xb_ref[...], None, sg_ref, su_ref, sd_ref)
        if has_over:
            acc = acc + yo_ref[...]
        for g in range(N_EXPERTS // MOE_EGROUP):
            p = _slot_one_hot(pos_ref, g * MOE_EGROUP, MOE_EGROUP, ncap, tm)
            ys = jnp.concatenate([ys_ref[e * MOE_CAP:e * MOE_CAP + ncap, :]
                                  for e in range(g * MOE_EGROUP, (g + 1) * MOE_EGROUP)], axis=0)
            acc = acc + lax.dot_general(p, ys, (((0,), (0,)), ((), ())), preferred_element_type=F32)
        y = _layer_norm(alpha * x_ref[...] + acc, g_ref[...], b_ref[...])
        y_ref[...] = y
        yb_ref[...] = y.astype(BF16)

    _when_slots_used(used_ref[t0 + pl.program_id(0)], run)


def _combine(ys, pos, used, x, xb, y_over, sg, su, sd, g, b, alpha, layer, row0, n):
    D = x.shape[1]
    F = sg.shape[2]
    assert row0 % MOE_TILE == 0 and n % MOE_TILE == 0
    t0 = row0 // MOE_TILE
    row_in = pl.BlockSpec((MOE_TILE, D), lambda i, u: (t0 + i, 0))
    row_out = pl.BlockSpec((MOE_TILE, D), lambda i, u: (i, 0))
    vec = pl.BlockSpec((1, D), lambda i, u: (0, 0))
    over = [] if y_over is None else [y_over]
    return pl.pallas_call(
        functools.partial(_combine_kernel, alpha=alpha, t0=t0, has_over=y_over is not None),
        grid_spec=pltpu.PrefetchScalarGridSpec(
            num_scalar_prefetch=1,
            grid=(n // MOE_TILE,),
            in_specs=[pl.BlockSpec((None, N_EXPERTS * MOE_CAP, D), lambda i, u: (t0 + i, 0, 0)),
                      pl.BlockSpec((None, N_EXPERTS, MOE_TILE), lambda i, u: (t0 + i, 0, 0)),
                      row_in, row_in] + [row_in] * len(over) + [
                      pl.BlockSpec((None, D, F), lambda i, u: (layer, 0, 0)),
                      pl.BlockSpec((None, D, F), lambda i, u: (layer, 0, 0)),
                      pl.BlockSpec((None, F, D), lambda i, u: (layer, 0, 0)),
                      vec, vec],
            out_specs=[row_out, row_out]),
        out_shape=[jax.ShapeDtypeStruct((n, D), F32), jax.ShapeDtypeStruct((n, D), BF16)],
        compiler_params=_params("parallel"),
        name="moe_combine",
    )(used, ys, pos, x, xb, *over, sg, su, sd, g.reshape(1, D), b.reshape(1, D))


def _dense_experts_kernel(xb_ref, wo_ref, wg_ref, wu_ref, wd_ref, y_ref):
    e = pl.program_id(1)

    @pl.when(e == 0)
    def _():
        y_ref[...] = jnp.zeros_like(y_ref)

    lane = lax.broadcasted_iota(jnp.int32, wo_ref.shape, 1)
    col = jnp.sum(jnp.where(lane == e, wo_ref[...], 0.0), axis=1, keepdims=True)
    y_ref[...] += _swiglu_rows(xb_ref[...], col, wg_ref, wu_ref, wd_ref)


def _dense_experts(xb, wo, wg, wu, wd, layer):
    T, D = xb.shape
    _, n_e, _, F = wg.shape
    tm = _pick_tile(T, 1024)
    return pl.pallas_call(
        _dense_experts_kernel,
        grid=(T // tm, n_e),
        in_specs=[pl.BlockSpec((tm, D), lambda i, e: (i, 0)),
                  pl.BlockSpec((tm, WC_LANES), lambda i, e: (i, 0)),
                  pl.BlockSpec((None, None, D, F), lambda i, e: (layer, e, 0, 0)),
                  pl.BlockSpec((None, None, D, F), lambda i, e: (layer, e, 0, 0)),
                  pl.BlockSpec((None, None, F, D), lambda i, e: (layer, e, 0, 0))],
        out_specs=pl.BlockSpec((tm, D), lambda i, e: (i, 0)),
        out_shape=jax.ShapeDtypeStruct((T, D), F32),
        compiler_params=_params("parallel", "arbitrary"),
        name="moe_dense_overflow",
    )(xb, wo, wg, wu, wd)


def _moe(xb, x, w_router, bias, wg, wu, wd, sg, su, sd, g, b, alpha, layer, ranges):
    wc, wo, pos, cnt = _router(x, w_router, bias)
    cnt = cnt[:, :, 0]
    tile_used = jnp.max(cnt, axis=1).astype(jnp.int32)
    ys = _experts(_dispatch(xb, wc, pos, tile_used), cnt, wg, wu, wd, layer)
    def combine(y_over):
        return [_combine(ys, pos, tile_used, x, xb, y_over, sg, su, sd, g, b, alpha, layer, row0, n)
                for row0, n in ranges]

    return lax.cond(jnp.max(cnt) > MOE_CAP,
                    lambda: combine(_dense_experts(xb, wo, wg, wu, wd, layer)),
                    lambda: combine(None))


def kernel(x_prompt, x_sample, cache_sb_k, cache_sb_v, state_hgrn, ln_in_g, ln_in_b, w_in, w_out, hg_norm, hg_lb_logits, ln1_g, ln1_b, w_router, router_bias, w_exp_gate, w_exp_up, w_exp_down, w_sh_gate, w_sh_up, w_sh_down, ln2_g, ln2_b):
    B, S, D = x_prompt.shape
    Bs, Ts, _ = x_sample.shape
    depth = w_in.shape[0]
    n_p = B * S
    alpha = (2 * depth) ** 0.25

    x, xb = _entry_ln(x_prompt.reshape(n_p, D), x_sample.reshape(Bs * Ts, D), ln_in_g, ln_in_b)

    sp, sd = [], []
    kp = vp = kd = vd = None
    for l in range(depth):
        w_in_b = w_in[l].astype(BF16)
        proj = _in_proj(xb, w_in_b)
        kb_p, vb_p, kp, vp = _kv_proj(xb, w_in_b, kp, vp, l, depth, 0, n_p)
        kb_s, vb_s, kd, vd = _kv_proj(xb, w_in_b, kd, vd, l, depth, n_p, Bs * Ts)
        o_sb = _sb_prompt(proj, kb_p, vb_p, B, S)
        o_sb = _sb_sample(proj, kb_s, vb_s, cache_sb_k, cache_sb_v, o_sb, l, n_p, Bs, Ts)
        o_hg, s_p = _hgrn(proj, hg_lb_logits, hg_norm[l], None, None, l, 0, B, S)
        o_hg, s_s = _hgrn(proj, hg_lb_logits, hg_norm[l], state_hgrn[l], o_hg, l, n_p, Bs, Ts)
        x, xb = _merge_out(proj, o_sb, o_hg, x, w_out[l].astype(BF16), ln1_g[l], ln1_b[l], alpha)
        last = l == depth - 1
        ranges = [(0, n_p), (n_p, Bs * Ts)] if last else [(0, n_p + Bs * Ts)]
        outs = _moe(xb, x, w_router[l], router_bias[l], w_exp_gate, w_exp_up, w_exp_down,
                    w_sh_gate, w_sh_up, w_sh_down, ln2_g[l], ln2_b[l], alpha, l, ranges)
        if not last:
            x, xb = outs[0]
        sp.append(s_p)
        sd.append(s_s)
    kv_p = (depth, B, S, N_HEADS, HEAD_DIM)
    kv_s = (depth, Bs, Ts, N_HEADS, HEAD_DIM)
    return (outs[0][0].reshape(B, S, D), outs[1][0].reshape(Bs, Ts, D),
            kp.reshape(kv_p), vp.reshape(kv_p), jnp.stack(sp),
            kd.reshape(kv_s), vd.reshape(kv_s), jnp.stack(sd))
```

```python
import functools
import math

import jax
import jax.numpy as jnp
from jax import lax
from jax.experimental import pallas as pl
from jax.experimental.pallas import tpu as pltpu

F32 = jnp.float32
BF16 = jnp.bfloat16

N_HEADS = 8
HEAD_DIM = 128
N_EXPERTS = 64
TOP_K = 8
N_GROUPS = 8
TOPK_GROUPS = 4
GROUP_SIZE = N_EXPERTS // N_GROUPS
ROUTE_SCALE = 2.5
NEG_BIG = -1e9
LB_FLOOR = 1e-30
LN_EPS = 1e-5
RMS_EPS = 1e-6
LOG2E = 1.4426950408889634
N_SPLITS = 9
P_Q, P_HQ, P_HF, P_HI, P_HG, P_GA, P_GB = range(7)

LANES = 128
SUBLANES = 8
VMEM_LIMIT = 48 * 1024 * 1024

HG_CHUNK = 128
HG_SUB = 32
SB_BLOCK = 256
SB_QBLOCKS = 2
SB_GROUP = 8
WC_LANES = 128
MOE_TILE = 256
ROUTER_TILES = 2
MOE_CAP = 64
MOE_CAP_FAST = 48
MOE_EGROUP = 16
MOE_TILES_PER_STEP = 24


def _pick_tile(n, pref):
    if n <= pref:
        return n
    for t in range(pref, 7, -1):
        if n % t == 0 and t % SUBLANES == 0:
            return t
    raise ValueError(f"no tile for {n}")


def _params(*sem):
    return pltpu.CompilerParams(dimension_semantics=sem, vmem_limit_bytes=VMEM_LIMIT)


def _layer_norm(x, g, b):
    mu = jnp.mean(x, axis=-1, keepdims=True)
    xc = x - mu
    var = jnp.mean(xc * xc, axis=-1, keepdims=True)
    return xc * lax.rsqrt(var + LN_EPS) * g + b


def _sigmoid(x):
    return 1.0 / (1.0 + jnp.exp(-x))


def _ln_kernel(xp_ref, xs_ref, g_ref, b_ref, y_ref, yb_ref, *, p_tiles):
    x = jnp.where(pl.program_id(0) < p_tiles, xp_ref[...], xs_ref[...])
    y = _layer_norm(x, g_ref[...], b_ref[...])
    y_ref[...] = y
    yb_ref[...] = y.astype(BF16)


def _entry_ln(xp, xs, g, b):
    n_p, D = xp.shape
    n_s = xs.shape[0]
    tm = _pick_tile(math.gcd(n_p, n_s), 512)
    p_tiles = n_p // tm
    T = n_p + n_s
    return pl.pallas_call(
        functools.partial(_ln_kernel, p_tiles=p_tiles),
        grid=(T // tm,),
        in_specs=[pl.BlockSpec((tm, D), lambda i: (jnp.minimum(i, p_tiles - 1), 0)),
                  pl.BlockSpec((tm, D), lambda i: (jnp.maximum(i - p_tiles, 0), 0)),
                  pl.BlockSpec((1, D), lambda i: (0, 0)),
                  pl.BlockSpec((1, D), lambda i: (0, 0))],
        out_specs=[pl.BlockSpec((tm, D), lambda i: (i, 0)),
                   pl.BlockSpec((tm, D), lambda i: (i, 0))],
        out_shape=[jax.ShapeDtypeStruct((T, D), F32), jax.ShapeDtypeStruct((T, D), BF16)],
        compiler_params=_params("parallel"),
        name="entry_ln",
    )(xp, xs, g.reshape(1, D), b.reshape(1, D))


def _in_proj_kernel(x_ref, w_ref, o_ref):
    o_ref[...] = jnp.dot(x_ref[...], w_ref[...], preferred_element_type=F32)


def _in_proj(xb, w_in_b):
    T, D = xb.shape
    tm = _pick_tile(T, 1024)
    return pl.pallas_call(
        _in_proj_kernel,
        grid=(N_SPLITS - 2, T // tm),
        in_specs=[pl.BlockSpec((tm, D), lambda j, i: (i, 0)),
                  pl.BlockSpec((D, D), lambda j, i: (0, jnp.where(j >= 1, j + 2, j)))],
        out_specs=pl.BlockSpec((None, tm, D), lambda j, i: (j, i, 0)),
        out_shape=jax.ShapeDtypeStruct((N_SPLITS - 2, T, D), F32),
        compiler_params=_params("parallel", "parallel"),
        name="in_proj",
    )(xb, w_in_b)


def _kv_proj_kernel(x_ref, wk_ref, wv_ref, *refs):
    kb_ref, vb_ref, ko_ref, vo_ref = refs[-4:]
    tm = x_ref.shape[0]
    x = x_ref[...]
    for w_ref, b_ref, o_ref in ((wk_ref, kb_ref, ko_ref), (wv_ref, vb_ref, vo_ref)):
        y = jnp.dot(x, w_ref[...], preferred_element_type=F32)
        b_ref[...] = y.astype(BF16)
        for h in range(N_HEADS):
            o_ref[pl.ds(h, tm, stride=N_HEADS), :] = y[:, h * HEAD_DIM:(h + 1) * HEAD_DIM]


def _kv_proj(xb, w_in_b, k_prev, v_prev, layer, depth, row0, n):
    D = xb.shape[1]
    tm = _pick_tile(n, 512)
    assert row0 % tm == 0
    rb = row0 // tm
    out_spec = pl.BlockSpec((None, tm * N_HEADS, HEAD_DIM), lambda i: (layer, i, 0))
    out_sds = jax.ShapeDtypeStruct((depth, n * N_HEADS, HEAD_DIM), F32)
    args = [xb, w_in_b, w_in_b]
    in_specs = [pl.BlockSpec((tm, D), lambda i: (rb + i, 0)),
                pl.BlockSpec((D, D), lambda i: (0, 1)),
                pl.BlockSpec((D, D), lambda i: (0, 2))]
    aliases = {}
    if k_prev is not None:
        args += [k_prev, v_prev]
        in_specs += [pl.BlockSpec(memory_space=pl.ANY)] * 2
        aliases = {3: 2, 4: 3}
    return pl.pallas_call(
        _kv_proj_kernel,
        grid=(n // tm,),
        in_specs=in_specs,
        out_specs=[pl.BlockSpec((tm, D), lambda i: (i, 0)), pl.BlockSpec((tm, D), lambda i: (i, 0)),
                   out_spec, out_spec],
        out_shape=[jax.ShapeDtypeStruct((n, D), BF16), jax.ShapeDtypeStruct((n, D), BF16),
                   out_sds, out_sds],
        input_output_aliases=aliases,
        compiler_params=_params("parallel"),
        name="kv_proj",
    )(*args)


def _neg_tri(tk):
    r = lax.broadcasted_iota(jnp.int32, (tk, tk), 0)
    c = lax.broadcasted_iota(jnp.int32, (tk, tk), 1)
    return jnp.where(r >= c, -1.0, 0.0).astype(BF16)


def _strict_lower(tq, tk, offset=0):
    return (lax.broadcasted_iota(jnp.int32, (tq, tk), 1) + offset
            < lax.broadcasted_iota(jnp.int32, (tq, tk), 0))


def _sb_z(qs, kb):
    return lax.dot_general(qs, kb, (((1,), (1,)), ((), ())), preferred_element_type=F32)


def _sb_cumsum_lhs(z, tk, mask):
    sp = jnp.maximum(z, 0.0) + jnp.log(1.0 + jnp.exp2(jnp.abs(z) * (-LOG2E)))
    if mask is not None:
        sp = jnp.where(mask, sp, 0.0)
    sp = sp.astype(BF16)
    n = z.shape[1] // tk
    return sp if n == 1 else jnp.concatenate([sp[:, a * tk:(a + 1) * tk] for a in range(n)], axis=0)


def _sb_weights(z, c, carry, tk, mask):
    tq = z.shape[0]
    n = z.shape[1] // tk
    ws = [None] * n
    for a in range(n - 1, -1, -1):
        ca = c[a * tq:(a + 1) * tq, :]
        w = jnp.exp2((z[:, a * tk:(a + 1) * tk] + ca + carry) * LOG2E)
        if mask is not None:
            w = jnp.where(mask, w, 0.0)
        ws[a] = w.astype(BF16)
        carry = carry + ca[:, 0:1]
    return (ws[0] if n == 1 else jnp.concatenate(ws, axis=1)), carry


def _sb_prompt_kernel(q_ref, kb_scr, vb_scr, o_ref, tri_scr, *, blk):
    i = pl.program_id(2)

    @pl.when(i == 0)
    def _():
        tri_scr[...] = _neg_tri(blk)

    qs = (q_ref[...] * (1.0 / math.sqrt(HEAD_DIM))).astype(BF16)

    def block_rows(j):
        return pl.ds(pl.multiple_of(j * blk, blk), blk)

    tq = q_ref.shape[0]
    own = tq // blk

    def sweep(j_lo, n, co, own_last):
        carry, o = co
        rows = [block_rows(j_lo + a) for a in range(n)]
        masks = [_strict_lower(tq, blk, (a - (n - own)) * blk) if own_last and a >= n - own else None
                 for a in range(n)]
        zs = [_sb_z(qs, kb_scr[r, :]) for r in rows]
        cs = [jnp.dot(_sb_cumsum_lhs(z, blk, m), tri_scr[...], preferred_element_type=F32)
              for z, m in zip(zs, masks)]
        ws = []
        for a in range(n - 1, -1, -1):
            w, carry = _sb_weights(zs[a], cs[a], carry, blk, masks[a])
            ws.append(w)
        if n == 1:
            return carry, o + jnp.dot(ws[0], vb_scr[rows[0], :], preferred_element_type=F32)
        v_desc = jnp.concatenate([vb_scr[rows[a], :] for a in range(n - 1, -1, -1)], axis=0)
        return carry, o + jnp.dot(jnp.concatenate(ws, axis=1), v_desc, preferred_element_type=F32)

    co = (jnp.zeros((tq, 1), F32), jnp.zeros((tq, HEAD_DIM), F32))
    earlier = i * own
    rem = earlier % SB_GROUP
    co = lax.switch(rem // own,
                    [functools.partial(lambda r, co: sweep(earlier - r, r + own, co, True), r)
                     for r in range(0, SB_GROUP, own)], co)
    co = lax.fori_loop(0, earlier // SB_GROUP,
                       lambda p, co: sweep(earlier - rem - SB_GROUP * (p + 1), SB_GROUP, co, False), co)
    o_ref[...] = co[1]


def _sb_prompt(proj, kb, vb, B, S):
    D = N_HEADS * HEAD_DIM
    t_all = proj.shape[1]
    blk = _pick_tile(S, SB_BLOCK)
    tq = SB_QBLOCKS * blk if S % (SB_QBLOCKS * blk) == 0 else blk
    assert SB_GROUP % (tq // blk) == 0
    nq = S // tq
    return pl.pallas_call(
        functools.partial(_sb_prompt_kernel, blk=blk),
        grid=(B, N_HEADS, nq),
        in_specs=[pl.BlockSpec((None, tq, HEAD_DIM), lambda b, h, i: (P_Q, b * nq + i, h)),
                  pl.BlockSpec((S, HEAD_DIM), lambda b, h, i: (b, h)),
                  pl.BlockSpec((S, HEAD_DIM), lambda b, h, i: (b, h))],
        out_specs=pl.BlockSpec((tq, HEAD_DIM), lambda b, h, i: (b * nq + i, h)),
        out_shape=jax.ShapeDtypeStruct((t_all, D), F32),
        scratch_shapes=[pltpu.VMEM((blk, blk), BF16)],
        compiler_params=_params("parallel", "parallel", "arbitrary"),
        name="sb_prompt",
    )(proj, kb, vb)


def _sb_sample_kernel(q_ref, k_ref, v_ref, pk_ref, pv_ref, o_in_ref, o_ref, *, pblk):
    del o_in_ref
    T = q_ref.shape[0]
    P = pk_ref.shape[0] // N_HEADS

    def head_rows(ref, h):
        return ref[pl.ds(h, P, stride=N_HEADS), :].astype(BF16)

    heads = range(N_HEADS)
    cols = [slice(h * HEAD_DIM, (h + 1) * HEAD_DIM) for h in heads]
    mask = _strict_lower(T, T)
    qs = [(q_ref[:, cols[h]] * (1.0 / math.sqrt(HEAD_DIM))).astype(BF16) for h in heads]
    z_new = [_sb_z(qs[h], k_ref[:, cols[h]]) for h in heads]
    z_past = [_sb_z(qs[h], head_rows(pk_ref, h)) for h in heads]
    lhs_new = jnp.concatenate([_sb_cumsum_lhs(z_new[h], T, mask) for h in heads], axis=0)
    lhs_past = jnp.concatenate([_sb_cumsum_lhs(z_past[h], pblk, None) for h in heads], axis=0)
    c_new = jnp.dot(lhs_new, _neg_tri(T), preferred_element_type=F32)
    c_past = jnp.dot(lhs_past, _neg_tri(pblk), preferred_element_type=F32)
    rows_past = (P // pblk) * T
    for h in heads:
        w_new, carry = _sb_weights(z_new[h], c_new[h * T:(h + 1) * T], jnp.zeros((T, 1), F32), T, mask)
        w_past, _ = _sb_weights(z_past[h], c_past[h * rows_past:(h + 1) * rows_past], carry, pblk, None)
        o_ref[:, cols[h]] = (jnp.dot(w_new, v_ref[:, cols[h]], preferred_element_type=F32)
                             + jnp.dot(w_past, head_rows(pv_ref, h), preferred_element_type=F32))


def _sb_sample(proj, kb, vb, cache_k, cache_v, o_sb, layer, row0, B, T):
    D = N_HEADS * HEAD_DIM
    depth, _, P = cache_k.shape[:3]
    pblk = _pick_tile(P, SB_BLOCK)
    rb = row0 // T
    cache_k = cache_k.reshape(depth, B, P * N_HEADS, HEAD_DIM)
    cache_v = cache_v.reshape(depth, B, P * N_HEADS, HEAD_DIM)
    cache_spec = pl.BlockSpec((None, None, P * N_HEADS, HEAD_DIM), lambda b: (layer, b, 0, 0))
    return pl.pallas_call(
        functools.partial(_sb_sample_kernel, pblk=pblk),
        grid=(B,),
        in_specs=[pl.BlockSpec((None, T, D), lambda b: (P_Q, rb + b, 0)),
                  pl.BlockSpec((T, D), lambda b: (b, 0)),
                  pl.BlockSpec((T, D), lambda b: (b, 0)),
                  cache_spec, cache_spec,
                  pl.BlockSpec(memory_space=pl.ANY)],
        out_specs=pl.BlockSpec((T, D), lambda b: (rb + b, 0)),
        out_shape=jax.ShapeDtypeStruct(o_sb.shape, F32),
        input_output_aliases={5: 0},
        compiler_params=_params("parallel"),
        name="sb_sample",
    )(proj, kb, vb, cache_k, cache_v, o_sb)


def _cumsum_rows(x):
    n = x.shape[0]
    h1 = x.astype(BF16)
    r1 = x - h1.astype(F32)
    h2 = r1.astype(BF16)
    h3 = (r1 - h2.astype(F32)).astype(BF16)
    row = lax.broadcasted_iota(jnp.int32, (n, 3 * n), 0)
    col = lax.broadcasted_iota(jnp.int32, (n, 3 * n), 1)
    col = jnp.where(col >= 2 * n, col - 2 * n, jnp.where(col >= n, col - n, col))
    tril3 = jnp.where(row >= col, 1.0, 0.0).astype(BF16)
    return jnp.dot(tril3, jnp.concatenate([h1, h2, h3], axis=0), preferred_element_type=F32)


def _hgrn_kernel(*refs, layer, has_s0):
    if has_s0:
        q_ref, f_ref, i_ref, g_ref, lbl_ref, norm_ref, s0_ref, _, o_ref, sf_ref, st_scr = refs
    else:
        q_ref, f_ref, i_ref, g_ref, lbl_ref, norm_ref, o_ref, sf_ref, st_scr = refs
    c = pl.program_id(1)
    C = q_ref.shape[0]
    n_sub = C // HG_SUB

    @pl.when(c == 0)
    def _():
        for h in range(N_HEADS):
            if has_s0:
                st_scr[h] = s0_ref[h].T
            else:
                st_scr[h] = jnp.zeros((HEAD_DIM, HEAD_DIM), F32)

    lg = lbl_ref[...]
    ex = jnp.exp(lg - jnp.max(lg, axis=0, keepdims=True))
    soft = ex / jnp.sum(ex, axis=0, keepdims=True)
    lb = jnp.zeros((1, lg.shape[1]), F32)
    for m in range(1, layer + 1):
        lb = lb + soft[m:m + 1, :]

    a = f_ref[...]
    u = jnp.exp(-jnp.abs(a))
    lbp = jnp.maximum(lb, LB_FLOOR)
    num = jnp.where(a >= 0.0, 1.0 + lbp * u, u + lbp)
    logf = jnp.minimum(jnp.log(num) - jnp.log(1.0 + u), 0.0)
    kk_all = 1.0 - jnp.exp(logf)
    b_all = _cumsum_rows(logf)
    qraw = q_ref[...]
    q_all = qraw * _sigmoid(qraw)
    graw = g_ref[...]
    gate_all = graw * _sigmoid(graw)
    v_all = i_ref[...]

    heads = range(N_HEADS)
    cols = [slice(h * HEAD_DIM, (h + 1) * HEAD_DIM) for h in heads]
    o_inter, scores, vbs = [], [], []
    for h in heads:
        b = b_all[:, cols[h]]
        q = q_all[:, cols[h]]
        kk = kk_all[:, cols[h]]
        vb = v_all[:, cols[h]].astype(BF16)
        st = st_scr[h]
        qt = (q * jnp.exp(b)).astype(BF16)
        o_inter.append(lax.dot_general(qt, st.astype(BF16), (((1,), (1,)), ((), ())),
                                       preferred_element_type=F32))
        sc_h = []
        for i in range(n_sub):
            lo_r, hi_r = i * HG_SUB, (i + 1) * HG_SUB
            if i == 0:
                ref_row = jnp.zeros((1, HEAD_DIM), F32)
            else:
                ref_row = b[lo_r - 1:lo_r, :]
            qh = (q[lo_r:hi_r] * jnp.exp(b[lo_r:hi_r] - ref_row)).astype(BF16)
            kh = (kk[0:hi_r] * jnp.exp(ref_row - b[0:hi_r])).astype(BF16)
            sc_h.append(lax.dot_general(qh, kh, (((1,), (1,)), ((), ())),
                                        preferred_element_type=F32))
        b_last = b[C - 1:C, :]
        ke = (kk * jnp.exp(b_last - b)).astype(BF16)
        upd = lax.dot_general(vb, ke, (((0,), (0,)), ((), ())), preferred_element_type=F32)
        st_scr[h] = st * jnp.exp(b_last) + upd
        scores.append(sc_h)
        vbs.append(vb)
    for h in heads:
        parts = []
        for i in range(n_sub):
            lo_r, hi_r = i * HG_SUB, (i + 1) * HG_SUB
            tpos = lax.broadcasted_iota(jnp.int32, (HG_SUB, hi_r), 0) + lo_r
            spos = lax.broadcasted_iota(jnp.int32, (HG_SUB, hi_r), 1)
            sc = jnp.where(spos <= tpos, scores[h][i], 0.0).astype(BF16)
            parts.append(jnp.dot(sc, vbs[h][0:hi_r], preferred_element_type=F32))
        o = o_inter[h] + jnp.concatenate(parts, axis=0)
        ms = jnp.mean(o * o, axis=-1, keepdims=True)
        o_ref[:, cols[h]] = o * lax.rsqrt(ms + RMS_EPS) * norm_ref[...] * gate_all[:, cols[h]]

    @pl.when(c == pl.num_programs(1) - 1)
    def _():
        for h in range(N_HEADS):
            sf_ref[h] = st_scr[h].T


def _hgrn(proj, lb_logits, hg_norm, s0, o_prev, layer, row0, B, T):
    D = N_HEADS * HEAD_DIM
    C = min(HG_CHUNK, T)
    assert T % C == 0 and row0 % C == 0 and C % HG_SUB == 0
    nc = T // C
    rb = row0 // C
    depth = lb_logits.shape[0]
    t_all = proj.shape[1]

    def act_spec(split):
        return pl.BlockSpec((None, C, D), lambda b, c: (split, rb + b * nc + c, 0))

    in_specs = [act_spec(P_HQ), act_spec(P_HF), act_spec(P_HI), act_spec(P_HG),
                pl.BlockSpec((depth, D), lambda b, c: (0, 0)),
                pl.BlockSpec((1, HEAD_DIM), lambda b, c: (0, 0))]
    args = [proj, proj, proj, proj, lb_logits, hg_norm.reshape(1, HEAD_DIM)]
    if s0 is not None:
        in_specs.append(pl.BlockSpec((None, N_HEADS, HEAD_DIM, HEAD_DIM), lambda b, c: (b, 0, 0, 0)))
        in_specs.append(pl.BlockSpec(memory_space=pl.ANY))
        args += [s0, o_prev]
    return pl.pallas_call(
        functools.partial(_hgrn_kernel, layer=layer, has_s0=s0 is not None),
        grid=(B, nc),
        in_specs=in_specs,
        out_specs=[pl.BlockSpec((C, D), lambda b, c: (rb + b * nc + c, 0)),
                   pl.BlockSpec((None, N_HEADS, HEAD_DIM, HEAD_DIM), lambda b, c: (b, 0, 0, 0))],
        out_shape=[jax.ShapeDtypeStruct((t_all, D), F32),
                   jax.ShapeDtypeStruct((B, N_HEADS, HEAD_DIM, HEAD_DIM), F32)],
        input_output_aliases={} if s0 is None else {len(args) - 1: 0},
        scratch_shapes=[pltpu.VMEM((N_HEADS, HEAD_DIM, HEAD_DIM), F32)],
        compiler_params=_params("parallel", "arbitrary"),
        name="hgrn2",
    )(*args)


def _merge_out_kernel(ga_ref, gb_ref, oa_ref, ob_ref, x_ref, w_ref, g_ref, b_ref, y_ref, yb_ref, *, alpha):
    merged = _sigmoid(ga_ref[...]) * oa_ref[...] + _sigmoid(gb_ref[...]) * ob_ref[...]
    r = alpha * x_ref[...] + jnp.dot(merged.astype(BF16), w_ref[...], preferred_element_type=F32)
    y = _layer_norm(r, g_ref[...], b_ref[...])
    y_ref[...] = y
    yb_ref[...] = y.astype(BF16)


def _merge_out(proj, o_sb, o_hg, x, w_out_b, g, b, alpha):
    T, D = x.shape
    tm = _pick_tile(T, 256)
    row = pl.BlockSpec((tm, D), lambda i: (i, 0))
    vec = pl.BlockSpec((1, D), lambda i: (0, 0))
    return pl.pallas_call(
        functools.partial(_merge_out_kernel, alpha=alpha),
        grid=(T // tm,),
        in_specs=[pl.BlockSpec((None, tm, D), lambda i: (P_GA, i, 0)),
                  pl.BlockSpec((None, tm, D), lambda i: (P_GB, i, 0)),
                  row, row, row,
                  pl.BlockSpec((D, D), lambda i: (0, 0)), vec, vec],
        out_specs=[row, row],
        out_shape=[jax.ShapeDtypeStruct((T, D), F32), jax.ShapeDtypeStruct((T, D), BF16)],
        compiler_params=_params("parallel"),
        name="merge_out",
    )(proj, proj, o_sb, o_hg, x, w_out_b, g.reshape(1, D), b.reshape(1, D))


def _router_kernel(x_ref, wr_ref, bias_ref, wc_ref, wo_ref, pos_ref, cnt_ref):
    tm = x_ref.shape[0]
    logits = lax.dot_general(wr_ref[...], x_ref[...], (((1,), (1,)), ((), ())),
                             precision=lax.Precision.HIGHEST, preferred_element_type=F32)
    s = _sigmoid(logits)
    sb = s + bias_ref[...]
    sub = lax.broadcasted_iota(jnp.int32, (GROUP_SIZE, tm), 0)
    gs = []
    for g in range(N_GROUPS):
        blk = sb[g * GROUP_SIZE:(g + 1) * GROUP_SIZE, :]
        m1 = jnp.max(blk, axis=0, keepdims=True)
        first = jnp.min(jnp.where(blk == m1, sub, GROUP_SIZE), axis=0, keepdims=True)
        m2 = jnp.max(jnp.where(sub == first, -jnp.inf, blk), axis=0, keepdims=True)
        gs.append(m1 + m2)
    masked = []
    for g in range(N_GROUPS):
        beat = jnp.zeros((1, tm), jnp.int32)
        for g2 in range(N_GROUPS):
            if g2 == g:
                continue
            if g2 < g:
                beat = beat + (gs[g2] >= gs[g]).astype(jnp.int32)
            else:
                beat = beat + (gs[g2] > gs[g]).astype(jnp.int32)
        keep = beat < TOPK_GROUPS
        masked.append(jnp.where(keep, sb[g * GROUP_SIZE:(g + 1) * GROUP_SIZE, :], NEG_BIG))
    m = jnp.concatenate(masked, axis=0)
    erow = lax.broadcasted_iota(jnp.int32, (N_EXPERTS, tm), 0)
    sel = jnp.zeros((N_EXPERTS, tm), jnp.bool_)
    for _ in range(TOP_K):
        mx = jnp.max(m, axis=0, keepdims=True)
        first = jnp.min(jnp.where(m == mx, erow, N_EXPERTS), axis=0, keepdims=True)
        hit = erow == first
        sel = jnp.logical_or(sel, hit)
        m = jnp.where(hit, -jnp.inf, m)
    w = jnp.where(sel, s, 0.0)
    w = w / jnp.sum(w, axis=0, keepdims=True) * ROUTE_SCALE
    sel_f = jnp.where(sel, 1.0, 0.0)
    t_row = lax.broadcasted_iota(jnp.int32, (tm, tm), 0)
    t_col = lax.broadcasted_iota(jnp.int32, (tm, tm), 1)
    same_tile = (t_row // MOE_TILE) == (t_col // MOE_TILE)
    earlier = jnp.where(jnp.logical_and(t_row < t_col, same_tile), 1.0, 0.0).astype(BF16)
    slot = jnp.dot(sel_f.astype(BF16), earlier, preferred_element_type=F32)
    fits = jnp.logical_and(sel, slot < MOE_CAP)
    zpad = jnp.zeros((WC_LANES - N_EXPERTS, tm), F32)
    wc_ref[...] = jnp.concatenate([jnp.where(fits, w, 0.0), zpad], axis=0).T
    wo_ref[...] = jnp.concatenate([jnp.where(fits, 0.0, w), zpad], axis=0).T
    pos = jnp.where(fits, slot, -1.0)
    for k in range(tm // MOE_TILE):
        lanes = slice(k * MOE_TILE, (k + 1) * MOE_TILE)
        pos_ref[k] = pos[:, lanes]
        cnt_ref[k] = jnp.broadcast_to(jnp.sum(sel_f[:, lanes], axis=1, keepdims=True),
                                      (N_EXPERTS, WC_LANES))


def _router(x, w_router, bias):
    T, D = x.shape
    assert T % MOE_TILE == 0
    nt = T // MOE_TILE
    k = ROUTER_TILES if nt % ROUTER_TILES == 0 else 1
    tm = k * MOE_TILE
    return pl.pallas_call(
        _router_kernel,
        grid=(nt // k,),
        in_specs=[pl.BlockSpec((tm, D), lambda i: (i, 0)),
                  pl.BlockSpec((N_EXPERTS, D), lambda i: (0, 0)),
                  pl.BlockSpec((N_EXPERTS, 1), lambda i: (0, 0))],
        out_specs=[pl.BlockSpec((tm, WC_LANES), lambda i: (i, 0)),
                   pl.BlockSpec((tm, WC_LANES), lambda i: (i, 0)),
                   pl.BlockSpec((k, N_EXPERTS, MOE_TILE), lambda i: (i, 0, 0)),
                   pl.BlockSpec((k, N_EXPERTS, WC_LANES), lambda i: (i, 0, 0))],
        out_shape=[jax.ShapeDtypeStruct((T, WC_LANES), F32),
                   jax.ShapeDtypeStruct((T, WC_LANES), F32),
                   jax.ShapeDtypeStruct((nt, N_EXPERTS, MOE_TILE), F32),
                   jax.ShapeDtypeStruct((nt, N_EXPERTS, WC_LANES), F32)],
        compiler_params=_params("parallel"),
        name="router",
    )(x, w_router.T, bias.reshape(N_EXPERTS, 1))


def _slot_one_hot(pos_ref, e0, n, ncap, tm):
    slot = lax.broadcasted_iota(jnp.int32, (ncap, tm), 0).astype(F32)
    return jnp.concatenate([jnp.where(pos_ref[e:e + 1, :] == slot, 1.0, 0.0).astype(BF16)
                            for e in range(e0, e0 + n)], axis=0)


def _when_slots_used(used, body):
    @pl.when(used <= MOE_CAP_FAST)
    def _():
        body(MOE_CAP_FAST)

    @pl.when(used > MOE_CAP_FAST)
    def _():
        body(MOE_CAP)


def _dispatch_kernel(used_ref, xb_ref, wc_ref, pos_ref, xs_ref):
    tm = xb_ref.shape[0]
    wc = wc_ref[...]
    hi = wc.astype(BF16)
    lo = (wc - hi.astype(F32)).astype(BF16)
    x_aug = jnp.concatenate([xb_ref[...], hi[:, :N_EXPERTS], lo[:, :N_EXPERTS]], axis=1)

    def run(ncap):
        for g in range(N_EXPERTS // MOE_EGROUP):
            p = _slot_one_hot(pos_ref, g * MOE_EGROUP, MOE_EGROUP, ncap, tm)
            rows = jnp.dot(p, x_aug, preferred_element_type=F32).astype(BF16)
            for k in range(MOE_EGROUP):
                r0 = (g * MOE_EGROUP + k) * MOE_CAP
                xs_ref[r0:r0 + ncap, :] = rows[k * ncap:(k + 1) * ncap, :]
                if ncap < MOE_CAP:
                    xs_ref[r0 + ncap:r0 + MOE_CAP, :] = jnp.zeros((MOE_CAP - ncap, x_aug.shape[1]), BF16)

    _when_slots_used(used_ref[pl.program_id(0)], run)


def _dispatch(xb, wc, pos, used):
    T, D = xb.shape
    nt = T // MOE_TILE
    return pl.pallas_call(
        _dispatch_kernel,
        grid_spec=pltpu.PrefetchScalarGridSpec(
            num_scalar_prefetch=1,
            grid=(nt,),
            in_specs=[pl.BlockSpec((MOE_TILE, D), lambda i, u: (i, 0)),
                      pl.BlockSpec((MOE_TILE, WC_LANES), lambda i, u: (i, 0)),
                      pl.BlockSpec((None, N_EXPERTS, MOE_TILE), lambda i, u: (i, 0, 0))],
            out_specs=pl.BlockSpec((None, N_EXPERTS * MOE_CAP, D + WC_LANES), lambda i, u: (i, 0, 0))),
        out_shape=jax.ShapeDtypeStruct((nt, N_EXPERTS * MOE_CAP, D + WC_LANES), BF16),
        compiler_params=_params("parallel"),
        name="moe_dispatch",
    )(used, xb, wc, pos)


def _swiglu_rows(x, col, wg_ref, wu_ref, wd_ref):
    hg = jnp.dot(x, wg_ref[...].astype(BF16), preferred_element_type=F32)
    hu = jnp.dot(x, wu_ref[...].astype(BF16), preferred_element_type=F32)
    h = hg * _sigmoid(hg) * hu
    if col is not None:
        h = h * col
    return jnp.dot(h.astype(BF16), wd_ref[...].astype(BF16), preferred_element_type=F32)


def _experts_kernel(used_ref, xs_ref, wg32_ref, wu32_ref, wd32_ref, ys_ref, wg_ref, wu_ref, wd_ref):
    e = pl.program_id(0)
    j = pl.program_id(1)

    @pl.when(j == 0)
    def _():
        wg_ref[...] = wg32_ref[...].astype(BF16)
        wu_ref[...] = wu32_ref[...].astype(BF16)
        wd_ref[...] = wd32_ref[...].astype(BF16)

    tb, cap, width = xs_ref.shape
    d = width - WC_LANES

    def run(ncap):
        xa = xs_ref[:, 0:ncap, :].reshape(tb * ncap, width)
        aug = xa[:, d:].astype(F32)
        lane = lax.broadcasted_iota(jnp.int32, aug.shape, 1)
        mine = jnp.logical_or(lane == e, lane == e + N_EXPERTS)
        col = jnp.sum(jnp.where(mine, aug, 0.0), axis=1, keepdims=True)
        y = _swiglu_rows(xa[:, :d], col, wg_ref, wu_ref, wd_ref)
        ys_ref[:, 0:ncap, :] = y.astype(BF16).reshape(tb, ncap, d)
        if ncap < cap:
            ys_ref[:, ncap:cap, :] = jnp.zeros((tb, cap - ncap, d), BF16)

    _when_slots_used(used_ref[e * pl.num_programs(1) + j], run)


def _experts(xs, cnt, wg, wu, wd, layer):
    nt, _, width = xs.shape
    _, n_e, D, F = wg.shape
    tb = _pick_tile(nt, MOE_TILES_PER_STEP) if nt > MOE_TILES_PER_STEP else nt
    if nt % tb:
        tb = 1
    steps = nt // tb
    used = jnp.max(cnt.reshape(steps, tb, n_e), axis=1).T.reshape(n_e * steps).astype(jnp.int32)
    return pl.pallas_call(
        _experts_kernel,
        grid_spec=pltpu.PrefetchScalarGridSpec(
            num_scalar_prefetch=1,
            grid=(n_e, steps),
            in_specs=[pl.BlockSpec((tb, MOE_CAP, width), lambda e, j, u: (j, e, 0)),
                      pl.BlockSpec((None, None, D, F), lambda e, j, u: (layer, e, 0, 0)),
                      pl.BlockSpec((None, None, D, F), lambda e, j, u: (layer, e, 0, 0)),
                      pl.BlockSpec((None, None, F, D), lambda e, j, u: (layer, e, 0, 0))],
            out_specs=pl.BlockSpec((tb, MOE_CAP, D), lambda e, j, u: (j, e, 0)),
            scratch_shapes=[pltpu.VMEM((D, F), BF16), pltpu.VMEM((D, F), BF16),
                            pltpu.VMEM((F, D), BF16)]),
        out_shape=jax.ShapeDtypeStruct((nt, n_e * MOE_CAP, D), BF16),
        compiler_params=_params("parallel", "arbitrary"),
        name="moe_experts",
    )(used, xs, wg, wu, wd)


def _combine_kernel(used_ref, ys_ref, pos_ref, x_ref, xb_ref, *refs, alpha, t0, has_over):
    if has_over:
        yo_ref, sg_ref, su_ref, sd_ref, g_ref, b_ref, y_ref, yb_ref = refs
    else:
        sg_ref, su_ref, sd_ref, g_ref, b_ref, y_ref, yb_ref = refs
    tm = x_ref.shape[0]

    def run(ncap):
        acc = _swiglu_rows(xb_ref[...], None, sg_ref, su_ref, sd_ref)
        if has_over:
            acc = acc + yo_ref[...]
        for g in range(N_EXPERTS // MOE_EGROUP):
            p = _slot_one_hot(pos_ref, g * MOE_EGROUP, MOE_EGROUP, ncap, tm)
            ys = jnp.concatenate([ys_ref[e * MOE_CAP:e * MOE_CAP + ncap, :]
                                  for e in range(g * MOE_EGROUP, (g + 1) * MOE_EGROUP)], axis=0)
            acc = acc + lax.dot_general(p, ys, (((0,), (0,)), ((), ())), preferred_element_type=F32)
        y = _layer_norm(alpha * x_ref[...] + acc, g_ref[...], b_ref[...])
        y_ref[...] = y
        yb_ref[...] = y.astype(BF16)

    _when_slots_used(used_ref[t0 + pl.program_id(0)], run)


def _combine(ys, pos, used, x, xb, y_over, sg, su, sd, g, b, alpha, layer, row0, n):
    D = x.shape[1]
    F = sg.shape[2]
    assert row0 % MOE_TILE == 0 and n % MOE_TILE == 0
    t0 = row0 // MOE_TILE
    row_in = pl.BlockSpec((MOE_TILE, D), lambda i, u: (t0 + i, 0))
    row_out = pl.BlockSpec((MOE_TILE, D), lambda i, u: (i, 0))
    vec = pl.BlockSpec((1, D), lambda i, u: (0, 0))
    over = [] if y_over is None else [y_over]
    return pl.pallas_call(
        functools.partial(_combine_kernel, alpha=alpha, t0=t0, has_over=y_over is not None),
        grid_spec=pltpu.PrefetchScalarGridSpec(
            num_scalar_prefetch=1,
            grid=(n // MOE_TILE,),
            in_specs=[pl.BlockSpec((None, N_EXPERTS * MOE_CAP, D), lambda i, u: (t0 + i, 0, 0)),
                      pl.BlockSpec((None, N_EXPERTS, MOE_TILE), lambda i, u: (t0 + i, 0, 0)),
                      row_in, row_in] + [row_in] * len(over) + [
                      pl.BlockSpec((None, D, F), lambda i, u: (layer, 0, 0)),
                      pl.BlockSpec((None, D, F), lambda i, u: (layer, 0, 0)),
                      pl.BlockSpec((None, F, D), lambda i, u: (layer, 0, 0)),
                      vec, vec],
            out_specs=[row_out, row_out]),
        out_shape=[jax.ShapeDtypeStruct((n, D), F32), jax.ShapeDtypeStruct((n, D), BF16)],
        compiler_params=_params("parallel"),
        name="moe_combine",
    )(used, ys, pos, x, xb, *over, sg, su, sd, g.reshape(1, D), b.reshape(1, D))


def _dense_experts_kernel(xb_ref, wo_ref, wg_ref, wu_ref, wd_ref, y_ref):
    e = pl.program_id(1)

    @pl.when(e == 0)
    def _():
        y_ref[...] = jnp.zeros_like(y_ref)

    lane = lax.broadcasted_iota(jnp.int32, wo_ref.shape, 1)
    col = jnp.sum(jnp.where(lane == e, wo_ref[...], 0.0), axis=1, keepdims=True)
    y_ref[...] += _swiglu_rows(xb_ref[...], col, wg_ref, wu_ref, wd_ref)


def _dense_experts(xb, wo, wg, wu, wd, layer):
    T, D = xb.shape
    _, n_e, _, F = wg.shape
    tm = _pick_tile(T, 1024)
    return pl.pallas_call(
        _dense_experts_kernel,
        grid=(T // tm, n_e),
        in_specs=[pl.BlockSpec((tm, D), lambda i, e: (i, 0)),
                  pl.BlockSpec((tm, WC_LANES), lambda i, e: (i, 0)),
                  pl.BlockSpec((None, None, D, F), lambda i, e: (layer, e, 0, 0)),
                  pl.BlockSpec((None, None, D, F), lambda i, e: (layer, e, 0, 0)),
                  pl.BlockSpec((None, None, F, D), lambda i, e: (layer, e, 0, 0))],
        out_specs=pl.BlockSpec((tm, D), lambda i, e: (i, 0)),
        out_shape=jax.ShapeDtypeStruct((T, D), F32),
        compiler_params=_params("parallel", "arbitrary"),
        name="moe_dense_overflow",
    )(xb, wo, wg, wu, wd)


def _moe(xb, x, w_router, bias, wg, wu, wd, sg, su, sd, g, b, alpha, layer, ranges):
    wc, wo, pos, cnt = _router(x, w_router, bias)
    cnt = cnt[:, :, 0]
    tile_used = jnp.max(cnt, axis=1).astype(jnp.int32)
    ys = _experts(_dispatch(xb, wc, pos, tile_used), cnt, wg, wu, wd, layer)
    def combine(y_over):
        return [_combine(ys, pos, tile_used, x, xb, y_over, sg, su, sd, g, b, alpha, layer, row0, n)
                for row0, n in ranges]

    return lax.cond(jnp.max(cnt) > MOE_CAP,
                    lambda: combine(_dense_experts(xb, wo, wg, wu, wd, layer)),
                    lambda: combine(None))


def kernel(x_prompt, x_sample, cache_sb_k, cache_sb_v, state_hgrn, ln_in_g, ln_in_b, w_in, w_out, hg_norm, hg_lb_logits, ln1_g, ln1_b, w_router, router_bias, w_exp_gate, w_exp_up, w_exp_down, w_sh_gate, w_sh_up, w_sh_down, ln2_g, ln2_b):
    B, S, D = x_prompt.shape
    Bs, Ts, _ = x_sample.shape
    depth = w_in.shape[0]
    n_p = B * S
    alpha = (2 * depth) ** 0.25

    x, xb = _entry_ln(x_prompt.reshape(n_p, D), x_sample.reshape(Bs * Ts, D), ln_in_g, ln_in_b)

    sp, sd = [], []
    kp = vp = kd = vd = None
    for l in range(depth):
        w_in_b = w_in[l].astype(BF16)
        proj = _in_proj(xb, w_in_b)
        kb_p, vb_p, kp, vp = _kv_proj(xb, w_in_b, kp, vp, l, depth, 0, n_p)
        kb_s, vb_s, kd, vd = _kv_proj(xb, w_in_b, kd, vd, l, depth, n_p, Bs * Ts)
        o_sb = _sb_prompt(proj, kb_p, vb_p, B, S)
        o_sb = _sb_sample(proj, kb_s, vb_s, cache_sb_k, cache_sb_v, o_sb, l, n_p, Bs, Ts)
        o_hg, s_p = _hgrn(proj, hg_lb_logits, hg_norm[l], None, None, l, 0, B, S)
        o_hg, s_s = _hgrn(proj, hg_lb_logits, hg_norm[l], state_hgrn[l], o_hg, l, n_p, Bs, Ts)
        x, xb = _merge_out(proj, o_sb, o_hg, x, w_out[l].astype(BF16), ln1_g[l], ln1_b[l], alpha)
        last = l == depth - 1
        ranges = [(0, n_p), (n_p, Bs * Ts)] if last else [(0, n_p + Bs * Ts)]
        outs = _moe(xb, x, w_router[l], router_bias[l], w_exp_gate, w_exp_up, w_exp_down,
                    w_sh_gate, w_sh_up, w_sh_down, ln2_g[l], ln2_b[l], alpha, l, ranges)
        if not last:
            x, xb = outs[0]
        sp.append(s_p)
        sd.append(s_s)
    kv_p = (depth, B, S, N_HEADS, HEAD_DIM)
    kv_s = (depth, Bs, Ts, N_HEADS, HEAD_DIM)
    return (outs[0][0].reshape(B, S, D), outs[1][0].reshape(Bs, Ts, D),
            kp.reshape(kv_p), vp.reshape(kv_p), jnp.stack(sp),
            kd.reshape(kv_s), vd.reshape(kv_s), jnp.stack(sd))
```
